```python
import jax, jax.numpy as jnp
from jax import lax
import numpy as np

D_MODEL = 1024
BATCH = 32
SEQ = 256
DEPTH = 4
DEC_BATCH = 8
DEC_SEQ = 2048
PAST_LEN = 256

GRID_W = 64
HEAD_DIM = 64
CONV_CH = D_MODEL // 4
CONV_K = 31
ATTN_W = (D_MODEL - CONV_CH) // 2
GQA_HEADS = ATTN_W // HEAD_DIM
GQA_KV_HEADS = GQA_HEADS // 3
GQA_GROUP = GQA_HEADS // GQA_KV_HEADS
NA_HEADS = ATTN_W // HEAD_DIM
D_FF = 4 * D_MODEL
NA_WIN_R = 8
NA_WIN_C = 16
NA_CB = 16
NA_KC = NA_CB + NA_WIN_C
Q_BLOCK = 128
ROPE_THETA = 10000.0
EPS = 1e-6
NEG_INF = -1e30

kernel_name = 'hybrid_conv_gqa_natten_prefix_dit'


def _rms(x, g):
    xf = x.astype(jnp.float32)
    y = xf * lax.rsqrt(jnp.mean(xf * xf, axis=-1, keepdims=True) + EPS)
    return (y * g.astype(jnp.float32)).astype(x.dtype)


def _layer_norm(x, g, b):
    xf = x.astype(jnp.float32)
    mu = jnp.mean(xf, axis=-1, keepdims=True)
    var = jnp.mean(jnp.square(xf - mu), axis=-1, keepdims=True)
    y = (xf - mu) * lax.rsqrt(var + EPS)
    return (y * g.astype(jnp.float32) + b.astype(jnp.float32)).astype(x.dtype)


def _modulation(cvec, w, b):
    m = (jax.nn.silu(cvec) @ w + b)[:, None, :]
    return jnp.split(m, 6, axis=-1)


def _axial_rope(x):
    n = x.shape[1]
    t = jnp.arange(n)
    positions = ((t // GRID_W).astype(jnp.float32), (t % GRID_W).astype(jnp.float32))
    half = HEAD_DIM // 2
    quarter = half // 2
    inv = 1.0 / (ROPE_THETA ** (jnp.arange(quarter, dtype=jnp.float32) * 2.0 / half))
    xf = x.astype(jnp.float32)
    parts = []
    for axis_i, p in enumerate(positions):
        seg = xf[..., axis_i * half:(axis_i + 1) * half]
        ang = p[:, None] * inv[None, :]
        cos = jnp.cos(ang)[None, :, None, :]
        sin = jnp.sin(ang)[None, :, None, :]
        s1, s2 = seg[..., :quarter], seg[..., quarter:]
        parts.append(s1 * cos - s2 * sin)
        parts.append(s2 * cos + s1 * sin)
    return jnp.concatenate(parts, axis=-1).astype(x.dtype)


def _blocked_attention(q, k, v):
    B, Lq, Hkv, G, Dh = q.shape
    nb = Lq // Q_BLOCK
    qb = jnp.moveaxis(q.reshape(B, nb, Q_BLOCK, Hkv, G, Dh), 1, 0)
    scale = Dh ** -0.5

    def one(qblk):
        s = jnp.einsum('bqhgd,bkhd->bhgqk', qblk, k, preferred_element_type=jnp.float32) * scale
        p = jax.nn.softmax(s, axis=-1).astype(v.dtype)
        return jnp.einsum('bhgqk,bkhd->bqhgd', p, v)

    out = lax.map(one, qb)
    return jnp.moveaxis(out, 0, 1).reshape(B, Lq, Hkv * G * Dh)


def _neighborhood_attention(q, k, v, kc, vc, rpb):
    B, S, H, Dh = q.shape
    rows = S // GRID_W
    wr = min(NA_WIN_R, rows)
    ncb = GRID_W // NA_CB
    qg = q.reshape(B, rows, ncb, NA_CB, H, Dh)
    kg = k.reshape(B, rows, GRID_W, H, Dh)
    vg = v.reshape(B, rows, GRID_W, H, Dh)
    m = np.arange(ncb)
    qcols = m[:, None] * NA_CB + np.arange(NA_CB)[None, :]
    kb = np.clip(m * NA_CB - NA_WIN_C // 2, 0, GRID_W - NA_KC)
    kcols = kb[:, None] + np.arange(NA_KC)[None, :]
    cs = np.clip(qcols - NA_WIN_C // 2, 0, GRID_W - NA_WIN_C)
    col_mask = (kcols[:, None, :] >= cs[..., None]) & (kcols[:, None, :] < cs[..., None] + NA_WIN_C)
    col_off = np.clip(kcols[:, None, :] - qcols[..., None] + NA_WIN_C - 1, 0, 2 * NA_WIN_C - 2)
    col_bias = rpb.astype(jnp.float32)[:, :, col_off]
    scale = Dh ** -0.5

    def one_row(r):
        rs = jnp.clip(r - wr // 2, 0, rows - wr)
        kr = lax.dynamic_slice_in_dim(kg, rs, wr, axis=1)[:, :, kcols]
        vr = lax.dynamic_slice_in_dim(vg, rs, wr, axis=1)[:, :, kcols]
        qr = lax.dynamic_index_in_dim(qg, r, axis=1, keepdims=False)
        row_off = rs + jnp.arange(wr) - r + NA_WIN_R - 1
        bias = jnp.take(col_bias, row_off, axis=1).transpose(0, 2, 3, 1, 4)
        s_loc = jnp.einsum('bmqhd,bamkhd->bhmqak', qr, kr, preferred_element_type=jnp.float32) * scale + bias[None]
        s_loc = jnp.where(col_mask[None, None, :, :, None, :], s_loc, NEG_INF)
        s_loc = s_loc.reshape(B, H, ncb, NA_CB, wr * NA_KC)
        s_ctx = jnp.einsum('bmqhd,bkhd->bhmqk', qr, kc, preferred_element_type=jnp.float32) * scale
        p = jax.nn.softmax(jnp.concatenate([s_loc, s_ctx], axis=-1), axis=-1).astype(v.dtype)
        p_loc = p[..., :wr * NA_KC].reshape(B, H, ncb, NA_CB, wr, NA_KC)
        p_ctx = p[..., wr * NA_KC:]
        return (jnp.einsum('bhmqak,bamkhd->bmqhd', p_loc, vr)
                + jnp.einsum('bhmqk,bkhd->bmqhd', p_ctx, vc))

    out = lax.map(one_row, jnp.arange(rows))
    return jnp.moveaxis(out, 0, 1).reshape(B, S, H * Dh)


def _conv_module(u, dw_w, dw_b, ln_g, ln_b):
    a, g = jnp.split(u, 2, axis=-1)
    h = a * jax.nn.sigmoid(g)
    h = lax.conv_general_dilated(h, dw_w[:, None, :], window_strides=(1,),
                                 padding=[(CONV_K // 2, CONV_K // 2)],
                                 dimension_numbers=('NWC', 'WIO', 'NWC'),
                                 feature_group_count=CONV_CH) + dw_b
    return jax.nn.silu(_layer_norm(h, ln_g, ln_b))


def _layer(x, mod, lw, ctx):
    (n1, n2, w_in, dw_w, dw_b, ln_g, ln_b, aqg, akg, nqg, nkg, rpb, w_out, w1, w2) = lw
    sh1, sc1, g1, sh2, sc2, g2 = mod
    B, L, _ = x.shape
    h = _rms(x, n1) * (1 + sc1) + sh1
    sizes = [2 * CONV_CH, GQA_HEADS * HEAD_DIM, GQA_KV_HEADS * HEAD_DIM, GQA_KV_HEADS * HEAD_DIM,
             NA_HEADS * HEAD_DIM, NA_HEADS * HEAD_DIM, NA_HEADS * HEAD_DIM]
    idx = [int(s) for s in np.cumsum(sizes)[:-1]]
    u_conv, qa, ka, va, qn, kn, vn = jnp.split(h @ w_in, idx, axis=-1)
    a_out = _conv_module(u_conv, dw_w, dw_b, ln_g, ln_b)
    qa = _rms(qa.reshape(B, L, GQA_HEADS, HEAD_DIM), aqg)
    ka = _rms(ka.reshape(B, L, GQA_KV_HEADS, HEAD_DIM), akg)
    va = va.reshape(B, L, GQA_KV_HEADS, HEAD_DIM)
    qn = _rms(qn.reshape(B, L, NA_HEADS, HEAD_DIM), nqg)
    kn = _rms(kn.reshape(B, L, NA_HEADS, HEAD_DIM), nkg)
    vn = vn.reshape(B, L, NA_HEADS, HEAD_DIM)
    if ctx is None:
        b_out = _blocked_attention(qa.reshape(B, L, GQA_KV_HEADS, GQA_GROUP, HEAD_DIM), ka, va)
        c_out = _blocked_attention(qn[:, :, :, None, :], kn, vn)
        new_ctx = (ka, va, kn, vn)
    else:
        ck_a, cv_a, ck_n, cv_n = ctx
        qa = _axial_rope(qa)
        ka = _axial_rope(ka)
        b_out = _blocked_attention(qa.reshape(B, L, GQA_KV_HEADS, GQA_GROUP, HEAD_DIM),
                                   jnp.concatenate([ka, ck_a], axis=1),
                                   jnp.concatenate([va, cv_a], axis=1))
        c_out = _neighborhood_attention(qn, kn, vn, ck_n, cv_n, rpb)
        new_ctx = None
    x = x + g1 * (jnp.concatenate([a_out, b_out, c_out], axis=-1) @ w_out)
    h = _rms(x, n2) * (1 + sc2) + sh2
    x = x + g2 * (jnp.square(jax.nn.relu(h @ w1)) @ w2)
    return x, new_ctx


def setup_inputs(seed: int = 0) -> dict:
    key = jax.random.key(seed)
    ks = jax.random.split(key, 32)
    f32 = jnp.float32
    nrm = lambda k, shape, s: jax.random.normal(k, shape, f32) * s
    D = D_MODEL
    in_w = 2 * CONV_CH + (GQA_HEADS + 2 * GQA_KV_HEADS) * HEAD_DIM + 3 * NA_HEADS * HEAD_DIM
    return {
        'x_prompt': nrm(ks[0], (BATCH, SEQ, D), 1.0),
        'x_sample': nrm(ks[1], (DEC_BATCH, DEC_SEQ, D), 1.0),
        'cache_attn_k': nrm(ks[2], (DEC_BATCH, DEPTH, PAST_LEN, GQA_KV_HEADS, HEAD_DIM), 1.0),
        'cache_attn_v': nrm(ks[3], (DEC_BATCH, DEPTH, PAST_LEN, GQA_KV_HEADS, HEAD_DIM), 1.0),
        'cache_na_k': nrm(ks[4], (DEC_BATCH, DEPTH, PAST_LEN, NA_HEADS, HEAD_DIM), 1.0),
        'cache_na_v': nrm(ks[5], (DEC_BATCH, DEPTH, PAST_LEN, NA_HEADS, HEAD_DIM), 1.0),
        'c': nrm(ks[6], (DEC_BATCH, D), 1.0),
        'c_ctx': nrm(ks[7], (D,), 1.0),
        'ada_w': nrm(ks[8], (DEPTH, D, 6 * D), 0.5 * D ** -0.5),
        'ada_b': nrm(ks[9], (DEPTH, 6 * D), 0.02),
        'norm1_g': 1.0 + nrm(ks[10], (DEPTH, D), 0.05),
        'norm2_g': 1.0 + nrm(ks[11], (DEPTH, D), 0.05),
        'w_in': nrm(ks[12], (DEPTH, D, in_w), D ** -0.5),
        'conv_dw_w': nrm(ks[13], (DEPTH, CONV_K, CONV_CH), CONV_K ** -0.5),
        'conv_dw_b': nrm(ks[14], (DEPTH, CONV_CH), 0.02),
        'conv_ln_g': 1.0 + nrm(ks[15], (DEPTH, CONV_CH), 0.05),
        'conv_ln_b': nrm(ks[16], (DEPTH, CONV_CH), 0.02),
        'attn_q_g': 1.0 + nrm(ks[17], (DEPTH, HEAD_DIM), 0.05),
        'attn_k_g': 1.0 + nrm(ks[18], (DEPTH, HEAD_DIM), 0.05),
        'na_q_g': 1.0 + nrm(ks[19], (DEPTH, HEAD_DIM), 0.05),
        'na_k_g': 1.0 + nrm(ks[20], (DEPTH, HEAD_DIM), 0.05),
        'na_rpb': nrm(ks[21], (DEPTH, NA_HEADS, 2 * NA_WIN_R - 1, 2 * NA_WIN_C - 1), 0.5),
        'w_out': nrm(ks[22], (DEPTH, D, D), D ** -0.5),
        'mlp_w1': nrm(ks[23], (DEPTH, D, D_FF), D ** -0.5),
        'mlp_w2': nrm(ks[24], (DEPTH, D_FF, D), D_FF ** -0.5),
    }


def reference(x_prompt, x_sample, cache_attn_k, cache_attn_v, cache_na_k, cache_na_v, c, c_ctx,
              ada_w, ada_b, norm1_g, norm2_g, w_in, conv_dw_w, conv_dw_b, conv_ln_g, conv_ln_b,
              attn_q_g, attn_k_g, na_q_g, na_k_g, na_rpb, w_out, mlp_w1, mlp_w2):
    xp = x_prompt
    xs = x_sample
    ak, av, nk, nv = [], [], [], []
    for l in range(DEPTH):
        lw = (norm1_g[l], norm2_g[l], w_in[l], conv_dw_w[l], conv_dw_b[l], conv_ln_g[l], conv_ln_b[l],
              attn_q_g[l], attn_k_g[l], na_q_g[l], na_k_g[l], na_rpb[l], w_out[l], mlp_w1[l], mlp_w2[l])
        mod_ctx = _modulation(c_ctx[None, :], ada_w[l], ada_b[l])
        xp, (k_a, v_a, k_n, v_n) = _layer(xp, mod_ctx, lw, None)
        ak.append(k_a)
        av.append(v_a)
        nk.append(k_n)
        nv.append(v_n)
        mod_lat = _modulation(c, ada_w[l], ada_b[l])
        ctx = (cache_attn_k[:, l], cache_attn_v[:, l], cache_na_k[:, l], cache_na_v[:, l])
        xs, _ = _layer(xs, mod_lat, lw, ctx)
    new_attn_k = jnp.stack(ak, axis=1)
    new_attn_v = jnp.stack(av, axis=1)
    new_na_k = jnp.stack(nk, axis=1)
    new_na_v = jnp.stack(nv, axis=1)
    return (xp, xs, new_attn_k, new_attn_v, new_na_k, new_na_v)
```

```python
import functools
import math

import numpy as np
import jax
import jax.numpy as jnp
from jax import lax
from jax.experimental import pallas as pl
from jax.experimental.pallas import tpu as pltpu

D_MODEL = 1024
BATCH = 32
SEQ = 256
DEPTH = 4
DEC_BATCH = 8
DEC_SEQ = 2048
PAST_LEN = 256
GRID_W = 64
GRID_H = DEC_SEQ // GRID_W
HEAD_DIM = 64
CONV_CH = D_MODEL // 4
CONV_K = 31
ATTN_W = (D_MODEL - CONV_CH) // 2
GQA_HEADS = ATTN_W // HEAD_DIM
GQA_KV_HEADS = GQA_HEADS // 3
GQA_GROUP = GQA_HEADS // GQA_KV_HEADS
NA_HEADS = ATTN_W // HEAD_DIM
D_FF = 4 * D_MODEL
NA_WIN_R = 8
NA_WIN_C = 16
ROPE_THETA = 10000.0
EPS = 1e-6
NEG_INF = -1e30

KV_W = GQA_KV_HEADS * HEAD_DIM
IN_W = 2 * CONV_CH + ATTN_W + 2 * KV_W + 3 * ATTN_W
_U0, _QA0, _KA0, _VA0, _QN0, _KN0, _VN0 = 0, 512, 896, 1024, 1152, 1536, 1920
LANES = 128
MOD_ROWS = 16
LOG2E = math.log2(math.e)
Q_SCALE = HEAD_DIM ** -0.5 * LOG2E
TM = 512
FF_CHUNK = 512
MIB = 1024 * 1024

_F32 = jnp.float32
_BF16 = jnp.bfloat16


def _params(semantics, vmem_mib):
    return pltpu.CompilerParams(dimension_semantics=semantics, vmem_limit_bytes=vmem_mib * MIB)


def _dot(a, b):
    return jnp.dot(a, b, preferred_element_type=_F32)


def _dot_nt(a, b):
    return lax.dot_general(a, b, (((1,), (1,)), ((), ())), preferred_element_type=_F32)


def _low_half(shape):
    return lax.broadcasted_iota(jnp.int32, shape, len(shape) - 1) % LANES < HEAD_DIM


def _mod_kernel(c_ref, w_ref, b_ref, o_ref):
    c = c_ref[...]
    s = (c * jax.nn.sigmoid(c)).astype(_BF16)
    o_ref[...] = _dot(s, w_ref[...].astype(_BF16)) + b_ref[...]


def _modulation(cvec, ada_w, ada_b):
    tn = 1536
    return pl.pallas_call(
        _mod_kernel,
        grid=(DEPTH, 6 * D_MODEL // tn),
        in_specs=[
            pl.BlockSpec((MOD_ROWS, D_MODEL), lambda l, j: (0, 0)),
            pl.BlockSpec((None, D_MODEL, tn), lambda l, j: (l, 0, j)),
            pl.BlockSpec((None, 1, tn), lambda l, j: (l, 0, j)),
        ],
        out_specs=pl.BlockSpec((None, MOD_ROWS, tn), lambda l, j: (l, 0, j)),
        out_shape=jax.ShapeDtypeStruct((DEPTH, MOD_ROWS, 6 * D_MODEL), _F32),
        compiler_params=_params(("arbitrary", "arbitrary"), 40),
        name="modulation",
    )(cvec, ada_w, ada_b.reshape(DEPTH, 1, 6 * D_MODEL))


def _mod_spec(layer, row_fn, part):
    return pl.BlockSpec((None, None, 1, D_MODEL), lambda i: (layer, row_fn(i), 0, part))


def _row_fn(sample, tm):
    tiles_per_seq = DEC_SEQ // tm
    if sample:
        return lambda i: 1 + i // tiles_per_seq
    return lambda i: 0


def _rope_tables():
    t = np.arange(DEC_SEQ)
    pos = (t // GRID_W, t % GRID_W)
    half = HEAD_DIM // 2
    quarter = half // 2
    inv = 1.0 / (ROPE_THETA ** (np.arange(quarter) * 2.0 / half))
    lane = np.arange(LANES)
    d = lane % HEAD_DIM
    axis = d // half
    second = (d % half) >= quarter
    p = np.where(axis[None, :] == 0, pos[0][:, None], pos[1][:, None]).astype(np.float64)
    ang = p * inv[d % quarter][None, :]
    cos, sin = np.cos(ang), np.sin(ang)
    sin_next = np.where(second[None, :], 0.0, -sin)
    sin_prev = np.where(second[None, :], sin, 0.0)
    return tuple(jnp.asarray(a, _F32) for a in (cos, sin_next, sin_prev))


def _in_proj_kernel(*refs, rope):
    if rope:
        (x_ref, sh_ref, sc_ref, n1_ref, w_ref, g_ref, cos_ref, sn_ref, sp_ref,
         u_ref, qa_ref, ka_ref, va_ref, qn_ref, kn_ref, vn_ref) = refs
    else:
        (x_ref, sh_ref, sc_ref, n1_ref, w_ref, g_ref,
         u_ref, qa_ref, ka_ref, va_ref, qn_ref, kn_ref, vn_ref) = refs
    x = x_ref[...]
    ms = jnp.mean(x * x, axis=-1, keepdims=True)
    h = (x * lax.rsqrt(ms + EPS)) * n1_ref[...]
    h = h * (1.0 + sc_ref[...]) + sh_ref[...]
    hb = h.astype(_BF16)

    r = lax.broadcasted_iota(jnp.int32, (LANES, LANES), 0) // HEAD_DIM
    c = lax.broadcasted_iota(jnp.int32, (LANES, LANES), 1) // HEAD_DIM
    seg_mean = jnp.where(r == c, 1.0 / HEAD_DIM, 0.0).astype(_BF16)

    def proj(c0, width):
        return _dot(hb, w_ref[:, c0:c0 + width])

    def head_norm(y, gain_row, rotary):
        for j in range(y.shape[1] // LANES):
            blk = y[:, j * LANES:(j + 1) * LANES]
            ss = _dot((blk * blk).astype(_BF16), seg_mean)
            blk = blk * lax.rsqrt(ss + EPS) * g_ref[gain_row:gain_row + 1, :]
            if rotary:
                blk = (blk * cos_ref[...]
                       + pltpu.roll(blk, LANES - 16, 1) * sn_ref[...]
                       + pltpu.roll(blk, 16, 1) * sp_ref[...])
            yield j, blk

    u_ref[...] = proj(_U0, 2 * CONV_CH).astype(u_ref.dtype)
    for j, blk in head_norm(proj(_QA0, ATTN_W), 0, rope):
        qa_ref[:, j * LANES:(j + 1) * LANES] = blk.astype(qa_ref.dtype)
    for j, blk in head_norm(proj(_KA0, KV_W), 1, rope):
        ka_ref[:, j * LANES:(j + 1) * LANES] = blk.astype(ka_ref.dtype)
    va_ref[...] = proj(_VA0, KV_W).astype(va_ref.dtype)
    for j, blk in head_norm(proj(_QN0, ATTN_W), 2, False):
        qn_ref[:, j * LANES:(j + 1) * LANES] = blk.astype(qn_ref.dtype)
    for j, blk in head_norm(proj(_KN0, ATTN_W), 3, False):
        kn_ref[:, j * LANES:(j + 1) * LANES] = blk.astype(kn_ref.dtype)
    vn_ref[...] = proj(_VN0, ATTN_W).astype(vn_ref.dtype)


def _in_proj(x, mod4, layer, norm1_g, w_in, gains, rope_tabs, *, sample):
    tokens = x.shape[0]
    tm = TM
    row = _row_fn(sample, tm)
    kv_dtype = _BF16 if sample else _F32
    in_specs = [
        pl.BlockSpec((tm, D_MODEL), lambda i: (i, 0)),
        _mod_spec(layer, row, 0),
        _mod_spec(layer, row, 1),
        pl.BlockSpec((None, 1, D_MODEL), lambda i: (layer, 0, 0)),
        pl.BlockSpec((None, D_MODEL, IN_W), lambda i: (layer, 0, 0)),
        pl.BlockSpec((None, 4, LANES), lambda i: (layer, 0, 0)),
    ]
    args = [x, mod4, mod4, norm1_g, w_in, gains]
    if sample:
        tiles_per_seq = DEC_SEQ // tm
        in_specs += [pl.BlockSpec((tm, LANES), lambda i: (i % tiles_per_seq, 0))] * 3
        args += list(rope_tabs)
    widths = (2 * CONV_CH, ATTN_W, KV_W, KV_W, ATTN_W, ATTN_W, ATTN_W)
    dtypes = (_BF16, _BF16, kv_dtype, kv_dtype, _BF16, kv_dtype, kv_dtype)
    return pl.pallas_call(
        functools.partial(_in_proj_kernel, rope=sample),
        grid=(tokens // tm,),
        in_specs=in_specs,
        out_specs=[pl.BlockSpec((tm, w), lambda i: (i, 0)) for w in widths],
        out_shape=[jax.ShapeDtypeStruct((tokens, w), dt) for w, dt in zip(widths, dtypes)],
        compiler_params=_params(("arbitrary",), 48),
        name="in_proj_sample" if sample else "in_proj_prompt",
    )(*args)


_CONV_ROWS = 64
_CONV_PAD = 16


def _conv_kernel(u_ref, w_ref, b_ref, g_ref, beta_ref, o_ref, hp_ref, *, seq):
    zeros = jnp.zeros((_CONV_PAD, CONV_CH), _F32)
    hp_ref[0:_CONV_PAD, :] = zeros
    hp_ref[_CONV_PAD + seq:2 * _CONV_PAD + seq, :] = zeros
    glu_rows = 256

    def glu_body(i, carry):
        t0 = pl.multiple_of(i * glu_rows, glu_rows)
        a = u_ref[pl.ds(t0, glu_rows), 0:CONV_CH].astype(_F32)
        g = u_ref[pl.ds(t0, glu_rows), CONV_CH:2 * CONV_CH].astype(_F32)
        hp_ref[pl.ds(_CONV_PAD + t0, glu_rows), :] = a * jax.nn.sigmoid(g)
        return carry

    lax.fori_loop(0, seq // glu_rows, glu_body, 0)

    first = _CONV_PAD - CONV_K // 2
    window = _CONV_ROWS + 32

    def conv_body(i, carry):
        t0 = pl.multiple_of(i * _CONV_ROWS, _CONV_ROWS)
        win = hp_ref[pl.ds(t0, window), :]
        acc = jnp.zeros((_CONV_ROWS, CONV_CH), _F32) + b_ref[...]
        for k in range(CONV_K):
            acc = acc + win[first + k:first + k + _CONV_ROWS, :] * w_ref[k:k + 1, :]
        mu = jnp.mean(acc, axis=-1, keepdims=True)
        cen = acc - mu
        var = jnp.mean(cen * cen, axis=-1, keepdims=True)
        y = cen * lax.rsqrt(var + EPS) * g_ref[...] + beta_ref[...]
        o_ref[pl.ds(t0, _CONV_ROWS), :] = (y * jax.nn.sigmoid(y)).astype(o_ref.dtype)
        return carry

    lax.fori_loop(0, seq // _CONV_ROWS, conv_body, 0)


def _conv(u, layer, dw_w, dw_b, ln_g, ln_b, *, seq):
    nb = u.shape[0] // seq
    u3 = u.reshape(nb, seq, 2 * CONV_CH)
    vec = pl.BlockSpec((None, 1, CONV_CH), lambda b: (layer, 0, 0))
    out = pl.pallas_call(
        functools.partial(_conv_kernel, seq=seq),
        grid=(nb,),
        in_specs=[
            pl.BlockSpec((None, seq, 2 * CONV_CH), lambda b: (b, 0, 0)),
            pl.BlockSpec((None, CONV_K, CONV_CH), lambda b: (layer, 0, 0)),
            vec, vec, vec,
        ],
        out_specs=pl.BlockSpec((None, seq, CONV_CH), lambda b: (b, 0, 0)),
        out_shape=jax.ShapeDtypeStruct((nb, seq, CONV_CH), _BF16),
        scratch_shapes=[pltpu.VMEM((seq + 2 * _CONV_PAD, CONV_CH), _F32)],
        compiler_params=_params(("arbitrary",), 32),
        name="conv_module",
    )(u3, dw_w, dw_b, ln_g, ln_b)
    return out.reshape(nb * seq, CONV_CH)


def _softmax_pv(score_blocks, value_blocks):
    m = None
    for s in score_blocks:
        sm = jnp.max(s, axis=-1, keepdims=True)
        m = sm if m is None else jnp.maximum(m, sm)
    denom = None
    out = None
    for s, v in zip(score_blocks, value_blocks):
        p = jnp.exp2(s - m)
        ps = jnp.sum(p, axis=-1, keepdims=True)
        denom = ps if denom is None else denom + ps
        pv = _dot(p.astype(_BF16), v)
        out = pv if out is None else out + pv
    return out / denom


def _gqa_key_variants(k_f32):
    low = _low_half(k_f32.shape)
    swapped = pltpu.roll(k_f32, HEAD_DIM, 1)
    zero = jnp.zeros_like(k_f32)
    return (jnp.where(low, k_f32, zero), jnp.where(low, zero, swapped),
            jnp.where(low, swapped, zero), jnp.where(low, zero, k_f32))


def _gqa_variant_ids(head):
    kv = head // GQA_GROUP
    parity = head % 2
    return 2 * kv + parity, (0 if kv == parity else 1)


def _prompt_attn_kernel(qa_ref, ka_ref, va_ref, qn_ref, kn_ref, vn_ref, b_ref, c_ref):
    low_out = _low_half((SEQ, LANES))
    k_var = [k.astype(_BF16) for k in _gqa_key_variants(ka_ref[...])]
    v = va_ref[...]
    v_var = (v.astype(_BF16), pltpu.roll(v, HEAD_DIM, 1).astype(_BF16))
    for p in range(GQA_HEADS // 2):
        cols = slice(p * LANES, (p + 1) * LANES)
        q = qa_ref[:, cols]
        outs = []
        for parity in range(2):
            kid, vid = _gqa_variant_ids(2 * p + parity)
            outs.append(_softmax_pv([_dot_nt(q, k_var[kid])], [v_var[vid]]))
        b_ref[:, cols] = jnp.where(low_out, outs[0], outs[1]).astype(b_ref.dtype)
    for p in range(NA_HEADS // 2):
        cols = slice(p * LANES, (p + 1) * LANES)
        q = qn_ref[:, cols]
        k = kn_ref[:, cols]
        zero = jnp.zeros_like(k)
        v = vn_ref[:, cols].astype(_BF16)
        outs = []
        for parity in range(2):
            k_sel = jnp.where(low_out, k, zero) if parity == 0 else jnp.where(low_out, zero, k)
            outs.append(_softmax_pv([_dot_nt(q, k_sel.astype(_BF16))], [v]))
        c_ref[:, cols] = jnp.where(low_out, outs[0], outs[1]).astype(c_ref.dtype)


def _prompt_attention(qa, ka, va, qn, kn, vn):
    def spec(w):
        return pl.BlockSpec((SEQ, w), lambda b: (b, 0))
    return pl.pallas_call(
        _prompt_attn_kernel,
        grid=(BATCH,),
        in_specs=[spec(ATTN_W), spec(KV_W), spec(KV_W), spec(ATTN_W), spec(ATTN_W), spec(ATTN_W)],
        out_specs=[spec(ATTN_W), spec(ATTN_W)],
        out_shape=[jax.ShapeDtypeStruct((BATCH * SEQ, ATTN_W), _BF16)] * 2,
        compiler_params=_params(("arbitrary",), 32),
        name="prompt_attention",
    )(qa, ka, va, qn, kn, vn)


_GQA_TQ = 256
_GQA_KEYS = DEC_SEQ + PAST_LEN


def _gqa_kernel(q_ref, k_ref, v_ref, kc_ref, vc_ref, o_ref, kvar_ref, vvar_ref):
    @pl.when(pl.program_id(1) == 0)
    def _():
        for src_k, src_v, off, n in ((k_ref, v_ref, 0, DEC_SEQ), (kc_ref, vc_ref, DEC_SEQ, PAST_LEN)):
            for idx, kv in enumerate(_gqa_key_variants(src_k[...].astype(_F32))):
                kvar_ref[idx, off:off + n, :] = kv.astype(_BF16)
            v = src_v[...].astype(_F32)
            vvar_ref[0, off:off + n, :] = v.astype(_BF16)
            vvar_ref[1, off:off + n, :] = pltpu.roll(v, HEAD_DIM, 1).astype(_BF16)

    low_out = _low_half((_GQA_TQ, LANES))
    for p in range(GQA_HEADS // 2):
        cols = slice(p * LANES, (p + 1) * LANES)
        q = q_ref[:, cols]
        outs = []
        for parity in range(2):
            kid, vid = _gqa_variant_ids(2 * p + parity)
            outs.append(_softmax_pv([_dot_nt(q, kvar_ref[kid])], [vvar_ref[vid]]))
        o_ref[:, cols] = jnp.where(low_out, outs[0], outs[1]).astype(o_ref.dtype)


def _sample_gqa(q, k, v, cache_k, cache_v, layer):
    tiles = DEC_SEQ // _GQA_TQ
    new_spec = pl.BlockSpec((DEC_SEQ, KV_W), lambda b, i: (b, 0))
    cache_spec = pl.BlockSpec((None, None, PAST_LEN, KV_W), lambda b, i: (b, layer, 0, 0))
    return pl.pallas_call(
        _gqa_kernel,
        grid=(DEC_BATCH, tiles),
        in_specs=[pl.BlockSpec((_GQA_TQ, ATTN_W), lambda b, i: (b * tiles + i, 0)),
                  new_spec, new_spec, cache_spec, cache_spec],
        out_specs=pl.BlockSpec((_GQA_TQ, ATTN_W), lambda b, i: (b * tiles + i, 0)),
        out_shape=jax.ShapeDtypeStruct((DEC_BATCH * DEC_SEQ, ATTN_W), _BF16),
        scratch_shapes=[pltpu.VMEM((4, _GQA_KEYS, KV_W), _BF16), pltpu.VMEM((2, _GQA_KEYS, KV_W), _BF16)],
        compiler_params=_params(("arbitrary", "arbitrary"), 48),
        name="sample_gqa",
    )(q, k, v, cache_k, cache_v)


_NA_ROWS = 8
_NA_KEYS = NA_WIN_R * GRID_W


def _na_bias_tables(rpb):
    c = np.arange(GRID_W)[:, None]
    kc = np.arange(GRID_W)[None, :]
    cs = np.clip(c - NA_WIN_C // 2, 0, GRID_W - NA_WIN_C)
    mask = (kc >= cs) & (kc < cs + NA_WIN_C)
    col_off = np.clip(kc - c + NA_WIN_C - 1, 0, 2 * NA_WIN_C - 2)
    t = jnp.where(mask, rpb.astype(_F32)[..., col_off] * LOG2E, NEG_INF)
    tabs = jnp.stack([t[:, :, d:d + NA_WIN_R] for d in range(NA_WIN_R)], axis=2)
    tabs = tabs.transpose(0, 1, 2, 4, 3, 5)
    return tabs.reshape(DEPTH, NA_HEADS, NA_WIN_R, GRID_W, _NA_KEYS)


def _na_kernel(q_ref, k_ref, v_ref, kc_ref, vc_ref, bias_ref, o_ref, ksel_ref, kcsel_ref, vcb_ref):
    rb = pl.program_id(1)

    @pl.when(rb == 0)
    def _():
        k = k_ref[...]
        low = _low_half(k.shape)
        zero = jnp.zeros_like(k)
        ksel_ref[0] = jnp.where(low, k, zero)
        ksel_ref[1] = jnp.where(low, zero, k)
        kc = kc_ref[...]
        lowc = _low_half(kc.shape)
        zc = jnp.zeros_like(kc)
        kcsel_ref[0] = jnp.where(lowc, kc, zc).astype(_BF16)
        kcsel_ref[1] = jnp.where(lowc, zc, kc).astype(_BF16)
        vcb_ref[...] = vc_ref[...].astype(_BF16)

    low_out = _low_half((GRID_W, LANES))

    def row_body(i, carry):
        r = rb * _NA_ROWS + i
        rs = jnp.clip(r - NA_WIN_R // 2, 0, GRID_H - NA_WIN_R)
        d0 = rs - r + NA_WIN_R - 1
        k0 = pl.multiple_of(rs * GRID_W, GRID_W)
        q0 = pl.multiple_of(i * GRID_W, GRID_W)
        for p in range(NA_HEADS // 2):
            cols = slice(p * LANES, (p + 1) * LANES)
            q = q_ref[pl.ds(q0, GRID_W), cols]
            vw = v_ref[pl.ds(k0, _NA_KEYS), cols]
            vc = vcb_ref[:, cols]
            outs = []
            for parity in range(2):
                kw = ksel_ref[parity, pl.ds(k0, _NA_KEYS), cols]
                s_loc = _dot_nt(q, kw) + bias_ref[2 * p + parity, d0]
                s_ctx = _dot_nt(q, kcsel_ref[parity, :, cols])
                outs.append(_softmax_pv([s_loc, s_ctx], [vw, vc]))
            o_ref[pl.ds(q0, GRID_W), cols] = jnp.where(low_out, outs[0], outs[1]).astype(o_ref.dtype)
        return carry

    lax.fori_loop(0, _NA_ROWS, row_body, 0)


def _sample_na(q, k, v, cache_k, cache_v, bias, layer):
    steps = GRID_H // _NA_ROWS
    tq = _NA_ROWS * GRID_W
    new_spec = pl.BlockSpec((DEC_SEQ, ATTN_W), lambda b, i: (b, 0))
    cache_spec = pl.BlockSpec((None, None, PAST_LEN, ATTN_W), lambda b, i: (b, layer, 0, 0))
    return pl.pallas_call(
        _na_kernel,
        grid=(DEC_BATCH, steps),
        in_specs=[pl.BlockSpec((tq, ATTN_W), lambda b, i: (b * steps + i, 0)),
                  new_spec, new_spec, cache_spec, cache_spec,
                  pl.BlockSpec((None, NA_HEADS, NA_WIN_R, GRID_W, _NA_KEYS), lambda b, i: (layer, 0, 0, 0, 0))],
        out_specs=pl.BlockSpec((tq, ATTN_W), lambda b, i: (b * steps + i, 0)),
        out_shape=jax.ShapeDtypeStruct((DEC_BATCH * DEC_SEQ, ATTN_W), _BF16),
        scratch_shapes=[pltpu.VMEM((2, DEC_SEQ, ATTN_W), _BF16),
                        pltpu.VMEM((2, PAST_LEN, ATTN_W), _BF16),
                        pltpu.VMEM((PAST_LEN, ATTN_W), _BF16)],
        compiler_params=_params(("arbitrary", "arbitrary"), 48),
        name="sample_na",
    )(q, k, v, cache_k, cache_v, bias)


def _out_mlp_kernel(a_ref, b_ref, c_ref, x_ref, g1_ref, sh2_ref, sc2_ref, g2_ref, n2_ref,
                    wo_ref, w1_ref, w2_ref, o_ref):
    b0 = CONV_CH
    c0 = CONV_CH + ATTN_W
    y = (_dot(a_ref[...], wo_ref[0:b0, :]) + _dot(b_ref[...], wo_ref[b0:c0, :])
         + _dot(c_ref[...], wo_ref[c0:D_MODEL, :]))
    x1 = x_ref[...] + g1_ref[...] * y
    ms = jnp.mean(x1 * x1, axis=-1, keepdims=True)
    h = (x1 * lax.rsqrt(ms + EPS)) * n2_ref[...]
    hb = (h * (1.0 + sc2_ref[...]) + sh2_ref[...]).astype(_BF16)
    acc = jnp.zeros(x1.shape, _F32)
    for j in range(D_FF // FF_CHUNK):
        t = jnp.maximum(_dot(hb, w1_ref[:, j * FF_CHUNK:(j + 1) * FF_CHUNK]), 0.0)
        acc = acc + _dot((t * t).astype(_BF16), w2_ref[j * FF_CHUNK:(j + 1) * FF_CHUNK, :])
    o_ref[...] = x1 + g2_ref[...] * acc


def _out_mlp(a, b, c, x, mod4, layer, norm2_g, w_out, w1, w2, *, sample):
    tokens = x.shape[0]
    tm = TM
    row = _row_fn(sample, tm)

    def tok(w):
        return pl.BlockSpec((tm, w), lambda i: (i, 0))

    return pl.pallas_call(
        _out_mlp_kernel,
        grid=(tokens // tm,),
        in_specs=[tok(CONV_CH), tok(ATTN_W), tok(ATTN_W), tok(D_MODEL),
                  _mod_spec(layer, row, 2), _mod_spec(layer, row, 3),
                  _mod_spec(layer, row, 4), _mod_spec(layer, row, 5),
                  pl.BlockSpec((None, 1, D_MODEL), lambda i: (layer, 0, 0)),
                  pl.BlockSpec((None, D_MODEL, D_MODEL), lambda i: (layer, 0, 0)),
                  pl.BlockSpec((None, D_MODEL, D_FF), lambda i: (layer, 0, 0)),
                  pl.BlockSpec((None, D_FF, D_MODEL), lambda i: (layer, 0, 0))],
        out_specs=tok(D_MODEL),
        out_shape=jax.ShapeDtypeStruct((tokens, D_MODEL), _F32),
        compiler_params=_params(("arbitrary",), 56),
        name="out_mlp",
    )(a, b, c, x, mod4, mod4, mod4, mod4, norm2_g, w_out, w1, w2)


def kernel(x_prompt, x_sample, cache_attn_k, cache_attn_v, cache_na_k, cache_na_v, c, c_ctx, ada_w, ada_b, norm1_g, norm2_g, w_in, conv_dw_w, conv_dw_b, conv_ln_g, conv_ln_b, attn_q_g, attn_k_g, na_q_g, na_k_g, na_rpb, w_out, mlp_w1, mlp_w2):
    cvec = jnp.concatenate(
        [c_ctx[None, :], c, jnp.zeros((MOD_ROWS - 1 - DEC_BATCH, D_MODEL), _F32)], axis=0)
    mod4 = _modulation(cvec, ada_w, ada_b).reshape(DEPTH, MOD_ROWS, 1, 6 * D_MODEL)

    def tile2(g):
        return jnp.tile(g, (1, LANES // HEAD_DIM))
    gains = jnp.stack([tile2(attn_q_g) * Q_SCALE, tile2(attn_k_g),
                       tile2(na_q_g) * Q_SCALE, tile2(na_k_g)], axis=1)
    rope_tabs = _rope_tables()
    na_bias = _na_bias_tables(na_rpb)

    w_in_b = w_in.astype(_BF16)
    w_out_b = w_out.astype(_BF16)
    w1_b = mlp_w1.astype(_BF16)
    w2_b = mlp_w2.astype(_BF16)
    n1 = norm1_g.reshape(DEPTH, 1, D_MODEL)
    n2 = norm2_g.reshape(DEPTH, 1, D_MODEL)
    dw_b = conv_dw_b.reshape(DEPTH, 1, CONV_CH)
    ln_g = conv_ln_g.reshape(DEPTH, 1, CONV_CH)
    ln_b = conv_ln_b.reshape(DEPTH, 1, CONV_CH)
    ck_a = cache_attn_k.reshape(DEC_BATCH, DEPTH, PAST_LEN, KV_W)
    cv_a = cache_attn_v.reshape(DEC_BATCH, DEPTH, PAST_LEN, KV_W)
    ck_n = cache_na_k.reshape(DEC_BATCH, DEPTH, PAST_LEN, ATTN_W)
    cv_n = cache_na_v.reshape(DEC_BATCH, DEPTH, PAST_LEN, ATTN_W)

    xp = x_prompt.reshape(BATCH * SEQ, D_MODEL)
    xs = x_sample.reshape(DEC_BATCH * DEC_SEQ, D_MODEL)
    new_ctx = ([], [], [], [])
    for layer in range(DEPTH):
        u, qa, ka, va, qn, kn, vn = _in_proj(xp, mod4, layer, n1, w_in_b, gains, None, sample=False)
        a_out = _conv(u, layer, conv_dw_w, dw_b, ln_g, ln_b, seq=SEQ)
        b_out, c_out = _prompt_attention(qa, ka, va, qn, kn, vn)
        xp = _out_mlp(a_out, b_out, c_out, xp, mod4, layer, n2, w_out_b, w1_b, w2_b, sample=False)
        for acc, t, heads in zip(new_ctx, (ka, va, kn, vn),
                                 (GQA_KV_HEADS, GQA_KV_HEADS, NA_HEADS, NA_HEADS)):
            acc.append(t.reshape(BATCH, SEQ, heads, HEAD_DIM))
        u, qa, ka, va, qn, kn, vn = _in_proj(xs, mod4, layer, n1, w_in_b, gains, rope_tabs, sample=True)
        a_out = _conv(u, layer, conv_dw_w, dw_b, ln_g, ln_b, seq=DEC_SEQ)
        b_out = _sample_gqa(qa, ka, va, ck_a, cv_a, layer)
        c_out = _sample_na(qn, kn, vn, ck_n, cv_n, na_bias, layer)
        xs = _out_mlp(a_out, b_out, c_out, xs, mod4, layer, n2, w_out_b, w1_b, w2_b, sample=True)

    outs = [jnp.stack(t, axis=1) for t in new_ctx]
    return (xp.reshape(BATCH, SEQ, D_MODEL), xs.reshape(DEC_BATCH, DEC_SEQ, D_MODEL), *outs)
```

```python
import functools
import math

import numpy as np
import jax
import jax.numpy as jnp
from jax import lax
from jax.experimental import pallas as pl
from jax.experimental.pallas import tpu as pltpu

D_MODEL = 1024
BATCH = 32
SEQ = 256
DEPTH = 4
DEC_BATCH = 8
DEC_SEQ = 2048
PAST_LEN = 256
GRID_W = 64
GRID_H = DEC_SEQ // GRID_W
HEAD_DIM = 64
CONV_CH = D_MODEL // 4
CONV_K = 31
ATTN_W = (D_MODEL - CONV_CH) // 2
GQA_HEADS = ATTN_W // HEAD_DIM
GQA_KV_HEADS = GQA_HEADS // 3
GQA_GROUP = GQA_HEADS // GQA_KV_HEADS
NA_HEADS = ATTN_W // HEAD_DIM
D_FF = 4 * D_MODEL
NA_WIN_R = 8
NA_WIN_C = 16
ROPE_THETA = 10000.0
EPS = 1e-6
NEG_INF = -1e30

KV_W = GQA_KV_HEADS * HEAD_DIM
IN_W = 2 * CONV_CH + ATTN_W + 2 * KV_W + 3 * ATTN_W
_U0, _QA0, _KA0, _VA0, _QN0, _KN0, _VN0 = 0, 512, 896, 1024, 1152, 1536, 1920
LANES = 128
MOD_ROWS = 16
LOG2E = math.log2(math.e)
Q_SCALE = HEAD_DIM ** -0.5 * LOG2E
TM = 512
FF_CHUNK = 512
MIB = 1024 * 1024

_F32 = jnp.float32
_BF16 = jnp.bfloat16


def _params(semantics, vmem_mib):
    return pltpu.CompilerParams(dimension_semantics=semantics, vmem_limit_bytes=vmem_mib * MIB)


def _dot(a, b):
    return jnp.dot(a, b, preferred_element_type=_F32)


def _dot_nt(a, b):
    return lax.dot_general(a, b, (((1,), (1,)), ((), ())), preferred_element_type=_F32)


def _low_half(shape):
    return lax.broadcasted_iota(jnp.int32, shape, len(shape) - 1) % LANES < HEAD_DIM


def _mod_kernel(c_ref, w_ref, b_ref, o_ref):
    c = c_ref[...]
    s = (c * jax.nn.sigmoid(c)).astype(_BF16)
    o_ref[...] = _dot(s, w_ref[...].astype(_BF16)) + b_ref[...]


def _modulation(cvec, ada_w, ada_b):
    tn = 1536
    return pl.pallas_call(
        _mod_kernel,
        grid=(DEPTH, 6 * D_MODEL // tn),
        in_specs=[
            pl.BlockSpec((MOD_ROWS, D_MODEL), lambda l, j: (0, 0)),
            pl.BlockSpec((None, D_MODEL, tn), lambda l, j: (l, 0, j)),
            pl.BlockSpec((None, 1, tn), lambda l, j: (l, 0, j)),
        ],
        out_specs=pl.BlockSpec((None, MOD_ROWS, tn), lambda l, j: (l, 0, j)),
        out_shape=jax.ShapeDtypeStruct((DEPTH, MOD_ROWS, 6 * D_MODEL), _F32),
        compiler_params=_params(("arbitrary", "arbitrary"), 40),
        name="modulation",
    )(cvec, ada_w, ada_b.reshape(DEPTH, 1, 6 * D_MODEL))


def _mod_spec(layer, row_fn, part):
    return pl.BlockSpec((None, None, 1, D_MODEL), lambda i: (layer, row_fn(i), 0, part))


def _row_fn(sample, tm):
    tiles_per_seq = DEC_SEQ // tm
    if sample:
        return lambda i: 1 + i // tiles_per_seq
    return lambda i: 0


def _rope_tables():
    t = np.arange(DEC_SEQ)
    pos = (t // GRID_W, t % GRID_W)
    half = HEAD_DIM // 2
    quarter = half // 2
    inv = 1.0 / (ROPE_THETA ** (np.arange(quarter) * 2.0 / half))
    lane = np.arange(LANES)
    d = lane % HEAD_DIM
    axis = d // half
    second = (d % half) >= quarter
    p = np.where(axis[None, :] == 0, pos[0][:, None], pos[1][:, None]).astype(np.float64)
    ang = p * inv[d % quarter][None, :]
    cos, sin = np.cos(ang), np.sin(ang)
    sin_next = np.where(second[None, :], 0.0, -sin)
    sin_prev = np.where(second[None, :], sin, 0.0)
    return tuple(jnp.asarray(a, _F32) for a in (cos, sin_next, sin_prev))


_NORM_BLK = 256
_GAIN_W = 2 * ATTN_W + KV_W + ATTN_W


def _in_proj_kernel(*refs, rope):
    if rope:
        (x_ref, sh_ref, sc_ref, n1_ref, w_ref, g_ref, cos_ref, sn_ref, sp_ref,
         u_ref, qa_ref, ka_ref, va_ref, qn_ref, kn_ref, vn_ref) = refs
    else:
        (x_ref, sh_ref, sc_ref, n1_ref, w_ref, g_ref,
         u_ref, qa_ref, ka_ref, va_ref, qn_ref, kn_ref, vn_ref) = refs
    x = x_ref[...]
    ms = jnp.mean(x * x, axis=-1, keepdims=True)
    h = (x * lax.rsqrt(ms + EPS)) * n1_ref[...]
    h = h * (1.0 + sc_ref[...]) + sh_ref[...]
    hb = h.astype(_BF16)

    r = lax.broadcasted_iota(jnp.int32, (_NORM_BLK, _NORM_BLK), 0) // HEAD_DIM
    c = lax.broadcasted_iota(jnp.int32, (_NORM_BLK, _NORM_BLK), 1) // HEAD_DIM
    seg_mean = jnp.where(r == c, 1.0 / HEAD_DIM, 0.0).astype(_BF16)

    def proj(c0, width):
        return _dot(hb, w_ref[:, c0:c0 + width])

    def rotate(blk):
        return (blk * cos_ref[...] + pltpu.roll(blk, LANES - 16, 1) * sn_ref[...]
                + pltpu.roll(blk, 16, 1) * sp_ref[...])

    def head_norm(c0, width, gain0, rotary):
        for b in range(width // _NORM_BLK):
            y = proj(c0 + b * _NORM_BLK, _NORM_BLK)
            ss = _dot((y * y).astype(_BF16), seg_mean)
            g0 = gain0 + b * _NORM_BLK
            y = y * lax.rsqrt(ss + EPS) * g_ref[:, g0:g0 + _NORM_BLK]
            for j in range(_NORM_BLK // LANES):
                blk = y[:, j * LANES:(j + 1) * LANES]
                yield (b * _NORM_BLK) // LANES + j, rotate(blk) if rotary else blk

    u = proj(_U0, 2 * CONV_CH)
    u_ref[...] = (u[:, 0:CONV_CH] * jax.nn.sigmoid(u[:, CONV_CH:2 * CONV_CH])).astype(u_ref.dtype)
    gqa_blocks = ATTN_W // LANES
    for j, blk in head_norm(_QA0, ATTN_W + KV_W, 0, rope):
        if j < gqa_blocks:
            qa_ref[:, j * LANES:(j + 1) * LANES] = blk.astype(qa_ref.dtype)
        else:
            ka_ref[...] = blk.astype(ka_ref.dtype)
    va_ref[...] = proj(_VA0, KV_W).astype(va_ref.dtype)
    for j, blk in head_norm(_QN0, 2 * ATTN_W, ATTN_W + KV_W, False):
        if j < gqa_blocks:
            qn_ref[:, j * LANES:(j + 1) * LANES] = blk.astype(qn_ref.dtype)
        else:
            jj = j - gqa_blocks
            kn_ref[:, jj * LANES:(jj + 1) * LANES] = blk.astype(kn_ref.dtype)
    vn_ref[...] = proj(_VN0, ATTN_W).astype(vn_ref.dtype)


def _in_proj(x, mod4, layer, norm1_g, w_in, gains, rope_tabs, *, sample):
    tokens = x.shape[0]
    tm = TM
    row = _row_fn(sample, tm)
    kv_dtype = _BF16 if sample else _F32
    in_specs = [
        pl.BlockSpec((tm, D_MODEL), lambda i: (i, 0)),
        _mod_spec(layer, row, 0),
        _mod_spec(layer, row, 1),
        pl.BlockSpec((None, 1, D_MODEL), lambda i: (layer, 0, 0)),
        pl.BlockSpec((None, D_MODEL, IN_W), lambda i: (layer, 0, 0)),
        pl.BlockSpec((None, 1, _GAIN_W), lambda i: (layer, 0, 0)),
    ]
    args = [x, mod4, mod4, norm1_g, w_in, gains]
    if sample:
        tiles_per_seq = DEC_SEQ // tm
        in_specs += [pl.BlockSpec((tm, LANES), lambda i: (i % tiles_per_seq, 0))] * 3
        args += list(rope_tabs)
    widths = (CONV_CH, ATTN_W, KV_W, KV_W, ATTN_W, ATTN_W, ATTN_W)
    dtypes = (_BF16, _BF16, kv_dtype, kv_dtype, _BF16, kv_dtype, kv_dtype)
    return pl.pallas_call(
        functools.partial(_in_proj_kernel, rope=sample),
        grid=(tokens // tm,),
        in_specs=in_specs,
        out_specs=[pl.BlockSpec((tm, w), lambda i: (i, 0)) for w in widths],
        out_shape=[jax.ShapeDtypeStruct((tokens, w), dt) for w, dt in zip(widths, dtypes)],
        compiler_params=_params(("arbitrary",), 48),
        name="in_proj_sample" if sample else "in_proj_prompt",
    )(*args)


_CONV_ROWS = 64
_CONV_PAD = 16
_SUBLANES = 8


def _conv_kernel(h_ref, w_ref, b_ref, g_ref, beta_ref, o_ref, hp_ref, *, seq):
    zeros = jnp.zeros((_CONV_PAD, CONV_CH), _F32)
    hp_ref[0:_CONV_PAD, :] = zeros
    hp_ref[_CONV_PAD + seq:2 * _CONV_PAD + seq, :] = zeros
    copy_rows = 256

    def copy_body(i, carry):
        t0 = pl.multiple_of(i * copy_rows, copy_rows)
        hp_ref[pl.ds(_CONV_PAD + t0, copy_rows), :] = h_ref[pl.ds(t0, copy_rows), :].astype(_F32)
        return carry

    lax.fori_loop(0, seq // copy_rows, copy_body, 0)

    first = _CONV_PAD - CONV_K // 2
    span = _CONV_ROWS + _SUBLANES

    def conv_body(i, carry):
        t0 = pl.multiple_of(i * _CONV_ROWS, _CONV_ROWS)
        halves = []
        for c0 in range(0, CONV_CH, LANES):
            lanes = slice(c0, c0 + LANES)
            acc = jnp.zeros((_CONV_ROWS, LANES), _F32) + b_ref[:, lanes]
            for s in range(_SUBLANES):
                part = None
                for a in range((first + CONV_K + _SUBLANES - 1) // _SUBLANES):
                    k = _SUBLANES * a + s - first
                    if 0 <= k < CONV_K:
                        term = hp_ref[pl.ds(t0 + _SUBLANES * a, span), lanes] * w_ref[k:k + 1, lanes]
                        part = term if part is None else part + term
                acc = acc + part[s:s + _CONV_ROWS, :]
            halves.append(acc)
        acc = jnp.concatenate(halves, axis=1)
        mu = jnp.mean(acc, axis=-1, keepdims=True)
        cen = acc - mu
        var = jnp.mean(cen * cen, axis=-1, keepdims=True)
        y = cen * lax.rsqrt(var + EPS) * g_ref[...] + beta_ref[...]
        o_ref[pl.ds(t0, _CONV_ROWS), :] = (y * jax.nn.sigmoid(y)).astype(o_ref.dtype)
        return carry

    lax.fori_loop(0, seq // _CONV_ROWS, conv_body, 0, unroll=2)


def _conv(u, layer, dw_w, dw_b, ln_g, ln_b, *, seq):
    nb = u.shape[0] // seq
    u3 = u.reshape(nb, seq, CONV_CH)
    vec = pl.BlockSpec((None, 1, CONV_CH), lambda b: (layer, 0, 0))
    out = pl.pallas_call(
        functools.partial(_conv_kernel, seq=seq),
        grid=(nb,),
        in_specs=[
            pl.BlockSpec((None, seq, CONV_CH), lambda b: (b, 0, 0)),
            pl.BlockSpec((None, CONV_K, CONV_CH), lambda b: (layer, 0, 0)),
            vec, vec, vec,
        ],
        out_specs=pl.BlockSpec((None, seq, CONV_CH), lambda b: (b, 0, 0)),
        out_shape=jax.ShapeDtypeStruct((nb, seq, CONV_CH), _BF16),
        scratch_shapes=[pltpu.VMEM((seq + 2 * _CONV_PAD, CONV_CH), _F32)],
        compiler_params=_params(("arbitrary",), 32),
        name="conv_module",
    )(u3, dw_w, dw_b, ln_g, ln_b)
    return out.reshape(nb * seq, CONV_CH)


def _softmax_pv(score_blocks, value_blocks):
    m = None
    for s in score_blocks:
        sm = jnp.max(s, axis=-1, keepdims=True)
        m = sm if m is None else jnp.maximum(m, sm)
    denom = None
    out = None
    for s, v in zip(score_blocks, value_blocks):
        p = jnp.exp2(s - m)
        ps = jnp.sum(p, axis=-1, keepdims=True)
        denom = ps if denom is None else denom + ps
        pv = _dot(p.astype(_BF16), v)
        out = pv if out is None else out + pv
    return out / denom


def _gqa_key_variants(k_f32):
    low = _low_half(k_f32.shape)
    swapped = pltpu.roll(k_f32, HEAD_DIM, 1)
    zero = jnp.zeros_like(k_f32)
    return (jnp.where(low, k_f32, zero), jnp.where(low, zero, swapped),
            jnp.where(low, swapped, zero), jnp.where(low, zero, k_f32))


def _gqa_variant_ids(head):
    kv = head // GQA_GROUP
    parity = head % 2
    return 2 * kv + parity, (0 if kv == parity else 1)


def _prompt_attn_kernel(qa_ref, ka_ref, va_ref, qn_ref, kn_ref, vn_ref, b_ref, c_ref):
    low_out = _low_half((SEQ, LANES))
    k_var = [k.astype(_BF16) for k in _gqa_key_variants(ka_ref[...])]
    v = va_ref[...]
    v_var = (v.astype(_BF16), pltpu.roll(v, HEAD_DIM, 1).astype(_BF16))
    for p in range(GQA_HEADS // 2):
        cols = slice(p * LANES, (p + 1) * LANES)
        q = qa_ref[:, cols]
        outs = []
        for parity in range(2):
            kid, vid = _gqa_variant_ids(2 * p + parity)
            outs.append(_softmax_pv([_dot_nt(q, k_var[kid])], [v_var[vid]]))
        b_ref[:, cols] = jnp.where(low_out, outs[0], outs[1]).astype(b_ref.dtype)
    for p in range(NA_HEADS // 2):
        cols = slice(p * LANES, (p + 1) * LANES)
        q = qn_ref[:, cols]
        k = kn_ref[:, cols]
        zero = jnp.zeros_like(k)
        v = vn_ref[:, cols].astype(_BF16)
        outs = []
        for parity in range(2):
            k_sel = jnp.where(low_out, k, zero) if parity == 0 else jnp.where(low_out, zero, k)
            outs.append(_softmax_pv([_dot_nt(q, k_sel.astype(_BF16))], [v]))
        c_ref[:, cols] = jnp.where(low_out, outs[0], outs[1]).astype(c_ref.dtype)


def _prompt_attention(qa, ka, va, qn, kn, vn):
    def spec(w):
        return pl.BlockSpec((SEQ, w), lambda b: (b, 0))
    return pl.pallas_call(
        _prompt_attn_kernel,
        grid=(BATCH,),
        in_specs=[spec(ATTN_W), spec(KV_W), spec(KV_W), spec(ATTN_W), spec(ATTN_W), spec(ATTN_W)],
        out_specs=[spec(ATTN_W), spec(ATTN_W)],
        out_shape=[jax.ShapeDtypeStruct((BATCH * SEQ, ATTN_W), _BF16)] * 2,
        compiler_params=_params(("arbitrary",), 32),
        name="prompt_attention",
    )(qa, ka, va, qn, kn, vn)


_GQA_TQ = 256
_GQA_KEYS = DEC_SEQ + PAST_LEN


def _gqa_kernel(q_ref, k_ref, v_ref, kc_ref, vc_ref, o_ref, kvar_ref, vvar_ref, s0_ref, s1_ref):
    @pl.when(pl.program_id(1) == 0)
    def _():
        for src_k, src_v, off, n in ((k_ref, v_ref, 0, DEC_SEQ), (kc_ref, vc_ref, DEC_SEQ, PAST_LEN)):
            for idx, kv in enumerate(_gqa_key_variants(src_k[...].astype(_F32))):
                kvar_ref[idx, off:off + n, :] = kv.astype(_BF16)
            v = src_v[...].astype(_F32)
            vvar_ref[0, off:off + n, :] = v.astype(_BF16)
            vvar_ref[1, off:off + n, :] = pltpu.roll(v, HEAD_DIM, 1).astype(_BF16)

    low_out = _low_half((_GQA_TQ, LANES))
    s_refs = (s0_ref, s1_ref)

    def scores(head):
        kid, _ = _gqa_variant_ids(head)
        p = head // 2
        s_refs[head % 2][...] = _dot_nt(q_ref[:, p * LANES:(p + 1) * LANES], kvar_ref[kid])

    scores(0)
    outs = []
    for head in range(GQA_HEADS):
        if head + 1 < GQA_HEADS:
            scores(head + 1)
        _, vid = _gqa_variant_ids(head)
        outs.append(_softmax_pv([s_refs[head % 2][...]], [vvar_ref[vid]]))
        if head % 2 == 1:
            p = head // 2
            o_ref[:, p * LANES:(p + 1) * LANES] = jnp.where(low_out, outs[0], outs[1]).astype(o_ref.dtype)
            outs = []


def _sample_gqa(q, k, v, cache_k, cache_v, layer):
    tiles = DEC_SEQ // _GQA_TQ
    new_spec = pl.BlockSpec((DEC_SEQ, KV_W), lambda b, i: (b, 0))
    cache_spec = pl.BlockSpec((None, None, PAST_LEN, KV_W), lambda b, i: (b, layer, 0, 0))
    return pl.pallas_call(
        _gqa_kernel,
        grid=(DEC_BATCH, tiles),
        in_specs=[pl.BlockSpec((_GQA_TQ, ATTN_W), lambda b, i: (b * tiles + i, 0)),
                  new_spec, new_spec, cache_spec, cache_spec],
        out_specs=pl.BlockSpec((_GQA_TQ, ATTN_W), lambda b, i: (b * tiles + i, 0)),
        out_shape=jax.ShapeDtypeStruct((DEC_BATCH * DEC_SEQ, ATTN_W), _BF16),
        scratch_shapes=[pltpu.VMEM((4, _GQA_KEYS, KV_W), _BF16), pltpu.VMEM((2, _GQA_KEYS, KV_W), _BF16),
                        pltpu.VMEM((_GQA_TQ, _GQA_KEYS), _F32), pltpu.VMEM((_GQA_TQ, _GQA_KEYS), _F32)],
        compiler_params=_params(("arbitrary", "arbitrary"), 48),
        name="sample_gqa",
    )(q, k, v, cache_k, cache_v)


_NA_GROUP = 4
_NA_GROUPS = GRID_H // _NA_GROUP
_NA_GROUPS_PER_STEP = 2
_NA_WIN_ROWS = 12
_NA_Q = _NA_GROUP * GRID_W
_NA_KEYS = _NA_WIN_ROWS * GRID_W
_NA_VARIANTS = 3


def _na_window_start(group):
    lo, hi = 0, GRID_H - _NA_WIN_ROWS
    start = _NA_GROUP * group - NA_WIN_R // 2
    if isinstance(group, int):
        return min(max(start, lo), hi)
    return jnp.clip(start, lo, hi)


def _na_bias_tables(rpb):
    c = np.arange(GRID_W)[:, None]
    kc = np.arange(GRID_W)[None, :]
    cs = np.clip(c - NA_WIN_C // 2, 0, GRID_W - NA_WIN_C)
    col_ok = (kc >= cs) & (kc < cs + NA_WIN_C)
    col_off = np.clip(kc - c + NA_WIN_C - 1, 0, 2 * NA_WIN_C - 2)
    t = jnp.where(col_ok, rpb.astype(_F32)[..., col_off] * LOG2E, NEG_INF)
    row_off = np.zeros((_NA_VARIANTS, _NA_GROUP, _NA_WIN_ROWS), np.int32)
    row_ok = np.zeros((_NA_VARIANTS, _NA_GROUP, _NA_WIN_ROWS), bool)
    for v, group in enumerate((0, 1, _NA_GROUPS - 1)):
        w0 = _na_window_start(group)
        for i in range(_NA_GROUP):
            r = _NA_GROUP * group + i
            rs = min(max(r - NA_WIN_R // 2, 0), GRID_H - NA_WIN_R)
            for j in range(_NA_WIN_ROWS):
                kr = w0 + j
                row_ok[v, i, j] = rs <= kr < rs + NA_WIN_R
                row_off[v, i, j] = min(max(kr - r + NA_WIN_R - 1, 0), 2 * NA_WIN_R - 2)
    tabs = t[:, :, row_off]
    tabs = jnp.where(row_ok[:, :, :, None, None], tabs, NEG_INF)
    tabs = tabs.transpose(0, 2, 1, 3, 5, 4, 6)
    return tabs.reshape(DEPTH, _NA_VARIANTS, NA_HEADS, _NA_Q, _NA_KEYS)


def _na_kernel(q_ref, k_ref, v_ref, kc_ref, vc_ref, bias_ref, o_ref,
               ksel_ref, kcsel_ref, vcb_ref, s0_ref, s1_ref):
    step = pl.program_id(1)

    @pl.when(step == 0)
    def _():
        k = k_ref[...]
        low = _low_half(k.shape)
        zero = jnp.zeros_like(k)
        ksel_ref[0] = jnp.where(low, k, zero)
        ksel_ref[1] = jnp.where(low, zero, k)
        kc = kc_ref[...]
        lowc = _low_half(kc.shape)
        zc = jnp.zeros_like(kc)
        kcsel_ref[0] = jnp.where(lowc, kc, zc).astype(_BF16)
        kcsel_ref[1] = jnp.where(lowc, zc, kc).astype(_BF16)
        vcb_ref[...] = vc_ref[...].astype(_BF16)

    low_out = _low_half((_NA_Q, LANES))
    s_refs = (s0_ref, s1_ref)
    units = [(g, h) for g in range(_NA_GROUPS_PER_STEP) for h in range(NA_HEADS)]

    def geometry(g):
        group = step * _NA_GROUPS_PER_STEP + g
        variant = jnp.where(group == 0, 0, jnp.where(group == _NA_GROUPS - 1, 2, 1))
        k0 = pl.multiple_of(_na_window_start(group) * GRID_W, _NA_GROUP * GRID_W)
        return variant, k0

    geo = [geometry(g) for g in range(_NA_GROUPS_PER_STEP)]

    def scores(u):
        g, h = units[u]
        variant, k0 = geo[g]
        p, parity = divmod(h, 2)
        cols = slice(p * LANES, (p + 1) * LANES)
        q = q_ref[g * _NA_Q:(g + 1) * _NA_Q, cols]
        s_ref = s_refs[u % 2]
        s_ref[:, 0:_NA_KEYS] = _dot_nt(q, ksel_ref[parity, pl.ds(k0, _NA_KEYS), cols]) + bias_ref[variant, h]
        s_ref[:, _NA_KEYS:_NA_KEYS + PAST_LEN] = _dot_nt(q, kcsel_ref[parity, :, cols])

    scores(0)
    outs = []
    for u, (g, h) in enumerate(units):
        if u + 1 < len(units):
            scores(u + 1)
        _, k0 = geo[g]
        p = h // 2
        cols = slice(p * LANES, (p + 1) * LANES)
        s = s_refs[u % 2][...]
        m = jnp.max(s, axis=-1, keepdims=True)
        e = jnp.exp2(s - m)
        denom = jnp.sum(e, axis=-1, keepdims=True)
        eb = e.astype(_BF16)
        pv = (_dot(eb[:, 0:_NA_KEYS], v_ref[pl.ds(k0, _NA_KEYS), cols])
              + _dot(eb[:, _NA_KEYS:_NA_KEYS + PAST_LEN], vcb_ref[:, cols]))
        outs.append(pv / denom)
        if h % 2 == 1:
            o_ref[g * _NA_Q:(g + 1) * _NA_Q, cols] = jnp.where(low_out, outs[0], outs[1]).astype(o_ref.dtype)
            outs = []


def _sample_na(q, k, v, cache_k, cache_v, bias, layer):
    steps = _NA_GROUPS // _NA_GROUPS_PER_STEP
    tq = _NA_GROUPS_PER_STEP * _NA_Q
    new_spec = pl.BlockSpec((DEC_SEQ, ATTN_W), lambda b, i: (b, 0))
    cache_spec = pl.BlockSpec((None, None, PAST_LEN, ATTN_W), lambda b, i: (b, layer, 0, 0))
    return pl.pallas_call(
        _na_kernel,
        grid=(DEC_BATCH, steps),
        in_specs=[pl.BlockSpec((tq, ATTN_W), lambda b, i: (b * steps + i, 0)),
                  new_spec, new_spec, cache_spec, cache_spec,
                  pl.BlockSpec((None, _NA_VARIANTS, NA_HEADS, _NA_Q, _NA_KEYS),
                               lambda b, i: (layer, 0, 0, 0, 0), pipeline_mode=pl.Buffered(1))],
        out_specs=pl.BlockSpec((tq, ATTN_W), lambda b, i: (b * steps + i, 0)),
        out_shape=jax.ShapeDtypeStruct((DEC_BATCH * DEC_SEQ, ATTN_W), _BF16),
        scratch_shapes=[pltpu.VMEM((2, DEC_SEQ, ATTN_W), _BF16),
                        pltpu.VMEM((2, PAST_LEN, ATTN_W), _BF16),
                        pltpu.VMEM((PAST_LEN, ATTN_W), _BF16),
                        pltpu.VMEM((_NA_Q, _NA_KEYS + PAST_LEN), _F32),
                        pltpu.VMEM((_NA_Q, _NA_KEYS + PAST_LEN), _F32)],
        compiler_params=_params(("arbitrary", "arbitrary"), 56),
        name="sample_na",
    )(q, k, v, cache_k, cache_v, bias)


def _out_mlp_kernel(a_ref, b_ref, c_ref, x_ref, g1_ref, sh2_ref, sc2_ref, g2_ref, n2_ref,
                    wo_ref, w1_ref, w2_ref, o_ref):
    b0 = CONV_CH
    c0 = CONV_CH + ATTN_W
    y = (_dot(a_ref[...], wo_ref[0:b0, :]) + _dot(b_ref[...], wo_ref[b0:c0, :])
         + _dot(c_ref[...], wo_ref[c0:D_MODEL, :]))
    x1 = x_ref[...] + g1_ref[...] * y
    ms = jnp.mean(x1 * x1, axis=-1, keepdims=True)
    h = (x1 * lax.rsqrt(ms + EPS)) * n2_ref[...]
    hb = (h * (1.0 + sc2_ref[...]) + sh2_ref[...]).astype(_BF16)
    acc = jnp.zeros(x1.shape, _F32)
    for j in range(D_FF // FF_CHUNK):
        t = jnp.maximum(_dot(hb, w1_ref[:, j * FF_CHUNK:(j + 1) * FF_CHUNK]), 0.0)
        acc = acc + _dot((t * t).astype(_BF16), w2_ref[j * FF_CHUNK:(j + 1) * FF_CHUNK, :])
    o_ref[...] = x1 + g2_ref[...] * acc


def _out_mlp(a, b, c, x, mod4, layer, norm2_g, w_out, w1, w2, *, sample):
    tokens = x.shape[0]
    tm = TM
    row = _row_fn(sample, tm)

    def tok(w):
        return pl.BlockSpec((tm, w), lambda i: (i, 0))

    return pl.pallas_call(
        _out_mlp_kernel,
        grid=(tokens // tm,),
        in_specs=[tok(CONV_CH), tok(ATTN_W), tok(ATTN_W), tok(D_MODEL),
                  _mod_spec(layer, row, 2), _mod_spec(layer, row, 3),
                  _mod_spec(layer, row, 4), _mod_spec(layer, row, 5),
                  pl.BlockSpec((None, 1, D_MODEL), lambda i: (layer, 0, 0)),
                  pl.BlockSpec((None, D_MODEL, D_MODEL), lambda i: (layer, 0, 0)),
                  pl.BlockSpec((None, D_MODEL, D_FF), lambda i: (layer, 0, 0)),
                  pl.BlockSpec((None, D_FF, D_MODEL), lambda i: (layer, 0, 0))],
        out_specs=tok(D_MODEL),
        out_shape=jax.ShapeDtypeStruct((tokens, D_MODEL), _F32),
        compiler_params=_params(("arbitrary",), 56),
        name="out_mlp",
    )(a, b, c, x, mod4, mod4, mod4, mod4, norm2_g, w_out, w1, w2)


def kernel(x_prompt, x_sample, cache_attn_k, cache_attn_v, cache_na_k, cache_na_v, c, c_ctx, ada_w, ada_b, norm1_g, norm2_g, w_in, conv_dw_w, conv_dw_b, conv_ln_g, conv_ln_b, attn_q_g, attn_k_g, na_q_g, na_k_g, na_rpb, w_out, mlp_w1, mlp_w2):
    cvec = jnp.concatenate(
        [c_ctx[None, :], c, jnp.zeros((MOD_ROWS - 1 - DEC_BATCH, D_MODEL), _F32)], axis=0)
    mod4 = _modulation(cvec, ada_w, ada_b).reshape(DEPTH, MOD_ROWS, 1, 6 * D_MODEL)

    def heads(g, n):
        return jnp.tile(g, (1, n))
    gains = jnp.concatenate(
        [heads(attn_q_g, GQA_HEADS) * Q_SCALE, heads(attn_k_g, GQA_KV_HEADS),
         heads(na_q_g, NA_HEADS) * Q_SCALE, heads(na_k_g, NA_HEADS)], axis=1)[:, None, :]
    rope_tabs = _rope_tables()
    na_bias = _na_bias_tables(na_rpb)

    w_in_b = w_in.astype(_BF16)
    w_out_b = w_out.astype(_BF16)
    w1_b = mlp_w1.astype(_BF16)
    w2_b = mlp_w2.astype(_BF16)
    n1 = norm1_g.reshape(DEPTH, 1, D_MODEL)
    n2 = norm2_g.reshape(DEPTH, 1, D_MODEL)
    dw_b = conv_dw_b.reshape(DEPTH, 1, CONV_CH)
    ln_g = conv_ln_g.reshape(DEPTH, 1, CONV_CH)
    ln_b = conv_ln_b.reshape(DEPTH, 1, CONV_CH)
    ck_a = cache_attn_k.reshape(DEC_BATCH, DEPTH, PAST_LEN, KV_W)
    cv_a = cache_attn_v.reshape(DEC_BATCH, DEPTH, PAST_LEN, KV_W)
    ck_n = cache_na_k.reshape(DEC_BATCH, DEPTH, PAST_LEN, ATTN_W)
    cv_n = cache_na_v.reshape(DEC_BATCH, DEPTH, PAST_LEN, ATTN_W)

    xp = x_prompt.reshape(BATCH * SEQ, D_MODEL)
    xs = x_sample.reshape(DEC_BATCH * DEC_SEQ, D_MODEL)
    new_ctx = ([], [], [], [])
    for layer in range(DEPTH):
        u, qa, ka, va, qn, kn, vn = _in_proj(xp, mod4, layer, n1, w_in_b, gains, None, sample=False)
        a_out = _conv(u, layer, conv_dw_w, dw_b, ln_g, ln_b, seq=SEQ)
        b_out, c_out = _prompt_attention(qa, ka, va, qn, kn, vn)
        xp = _out_mlp(a_out, b_out, c_out, xp, mod4, layer, n2, w_out_b, w1_b, w2_b, sample=False)
        for acc, t, heads in zip(new_ctx, (ka, va, kn, vn),
                                 (GQA_KV_HEADS, GQA_KV_HEADS, NA_HEADS, NA_HEADS)):
            acc.append(t.reshape(BATCH, SEQ, heads, HEAD_DIM))
        u, qa, ka, va, qn, kn, vn = _in_proj(xs, mod4, layer, n1, w_in_b, gains, rope_tabs, sample=True)
        a_out = _conv(u, layer, conv_dw_w, dw_b, ln_g, ln_b, seq=DEC_SEQ)
        b_out = _sample_gqa(qa, ka, va, ck_a, cv_a, layer)
        c_out = _sample_na(qn, kn, vn, ck_n, cv_n, na_bias, layer)
        xs = _out_mlp(a_out, b_out, c_out, xs, mod4, layer, n2, w_out_b, w1_b, w2_b, sample=True)

    outs = [jnp.stack(t, axis=1) for t in new_ctx]
    return (xp.reshape(BATCH, SEQ, D_MODEL), xs.reshape(DEC_BATCH, DEC_SEQ, D_MODEL), *outs)
```

```python
import functools
import math

import numpy as np
import jax
import jax.numpy as jnp
from jax import lax
from jax.experimental import pallas as pl
from jax.experimental.pallas import tpu as pltpu

D_MODEL = 1024
BATCH = 32
SEQ = 256
DEPTH = 4
DEC_BATCH = 8
DEC_SEQ = 2048
PAST_LEN = 256
GRID_W = 64
GRID_H = DEC_SEQ // GRID_W
HEAD_DIM = 64
CONV_CH = D_MODEL // 4
CONV_K = 31
ATTN_W = (D_MODEL - CONV_CH) // 2
GQA_HEADS = ATTN_W // HEAD_DIM
GQA_KV_HEADS = GQA_HEADS // 3
GQA_GROUP = GQA_HEADS // GQA_KV_HEADS
NA_HEADS = ATTN_W // HEAD_DIM
D_FF = 4 * D_MODEL
NA_WIN_R = 8
NA_WIN_C = 16
ROPE_THETA = 10000.0
EPS = 1e-6
NEG_INF = -1e30

KV_W = GQA_KV_HEADS * HEAD_DIM
IN_W = 2 * CONV_CH + ATTN_W + 2 * KV_W + 3 * ATTN_W
_U0, _QA0, _KA0, _VA0, _QN0, _KN0, _VN0 = 0, 512, 896, 1024, 1152, 1536, 1920
LANES = 128
SUBLANES = 8
MXU_W = 256
MOD_ROWS = 16
LOG2E = math.log2(math.e)
Q_SCALE = HEAD_DIM ** -0.5 * LOG2E
TM = 512
FF_CHUNK = 512
KV_BLK = 256
KV_BLOCKS = DEC_SEQ // KV_BLK
PAIR = 2 * HEAD_DIM
MIB = 1024 * 1024

assert SEQ == KV_BLK and PAST_LEN == KV_BLK and TM % KV_BLK == 0 and PAIR == LANES

_F32 = jnp.float32
_BF16 = jnp.bfloat16


def _params(semantics, vmem_mib):
    return pltpu.CompilerParams(dimension_semantics=semantics, vmem_limit_bytes=vmem_mib * MIB)


def _dot(a, b):
    return jnp.dot(a, b, preferred_element_type=_F32)


def _dot_nt(a, b):
    return lax.dot_general(a, b, (((1,), (1,)), ((), ())), preferred_element_type=_F32)


def _low_half(shape):
    return lax.broadcasted_iota(jnp.int32, shape, len(shape) - 1) % LANES < HEAD_DIM


def _mod_kernel(c_ref, w_ref, b_ref, o_ref):
    c = c_ref[...]
    s = (c * jax.nn.sigmoid(c)).astype(_BF16)
    o_ref[...] = _dot(s, w_ref[...].astype(_BF16)) + b_ref[...]


def _modulation(cvec, ada_w, ada_b):
    tn = 1536
    return pl.pallas_call(
        _mod_kernel,
        grid=(DEPTH, 6 * D_MODEL // tn),
        in_specs=[
            pl.BlockSpec((MOD_ROWS, D_MODEL), lambda l, j: (0, 0)),
            pl.BlockSpec((None, D_MODEL, tn), lambda l, j: (l, 0, j)),
            pl.BlockSpec((None, 1, tn), lambda l, j: (l, 0, j)),
        ],
        out_specs=pl.BlockSpec((None, MOD_ROWS, tn), lambda l, j: (l, 0, j)),
        out_shape=jax.ShapeDtypeStruct((DEPTH, MOD_ROWS, 6 * D_MODEL), _F32),
        compiler_params=_params(("arbitrary", "arbitrary"), 40),
        name="modulation",
    )(cvec, ada_w, ada_b.reshape(DEPTH, 1, 6 * D_MODEL))


def _mod_spec(layer, row_fn, part):
    return pl.BlockSpec((None, None, 1, D_MODEL), lambda i: (layer, row_fn(i), 0, part))


def _row_fn(sample, tm):
    tiles_per_seq = DEC_SEQ // tm
    if sample:
        return lambda i: 1 + i // tiles_per_seq
    return lambda i: 0


def _rope_tables():
    t = np.arange(DEC_SEQ)
    pos = (t // GRID_W, t % GRID_W)
    half = HEAD_DIM // 2
    quarter = half // 2
    inv = 1.0 / (ROPE_THETA ** (np.arange(quarter) * 2.0 / half))
    lane = np.arange(LANES)
    d = lane % HEAD_DIM
    axis = d // half
    second = (d % half) >= quarter
    p = np.where(axis[None, :] == 0, pos[0][:, None], pos[1][:, None]).astype(np.float64)
    ang = p * inv[d % quarter][None, :]
    cos, sin = np.cos(ang), np.sin(ang)
    sin_next = np.where(second[None, :], 0.0, -sin)
    sin_prev = np.where(second[None, :], sin, 0.0)
    return tuple(jnp.asarray(a, _F32) for a in (cos, sin_next, sin_prev))


_GAIN_W = 2 * ATTN_W + KV_W + ATTN_W
_N_KV_OUT = 4


def _in_proj_kernel(*refs, rope, n_alias):
    n_in = 9 if rope else 6
    if rope:
        x_ref, sh_ref, sc_ref, n1_ref, w_ref, g_ref, cos_ref, sn_ref, sp_ref = refs[:n_in]
    else:
        x_ref, sh_ref, sc_ref, n1_ref, w_ref, g_ref = refs[:n_in]
    u_ref, qa_ref, ka_ref, va_ref, qn_ref, kn_ref, vn_ref = refs[n_in + n_alias:]
    x = x_ref[...]
    ms = jnp.mean(x * x, axis=-1, keepdims=True)
    h = (x * lax.rsqrt(ms + EPS)) * n1_ref[...]
    h = h * (1.0 + sc_ref[...]) + sh_ref[...]
    hb = h.astype(_BF16)

    r = lax.broadcasted_iota(jnp.int32, (MXU_W, MXU_W), 0) // HEAD_DIM
    c = lax.broadcasted_iota(jnp.int32, (MXU_W, MXU_W), 1) // HEAD_DIM
    seg_mean = jnp.where(r == c, 1.0 / HEAD_DIM, 0.0).astype(_BF16)

    def proj(c0, width):
        return _dot(hb, w_ref[:, c0:c0 + width])

    def rotate(blk):
        return (blk * cos_ref[...] + pltpu.roll(blk, LANES - 16, 1) * sn_ref[...]
                + pltpu.roll(blk, 16, 1) * sp_ref[...])

    def head_norm(c0, width, gain0, rotary):
        for b in range(width // MXU_W):
            y = proj(c0 + b * MXU_W, MXU_W)
            ss = _dot((y * y).astype(_BF16), seg_mean)
            g0 = gain0 + b * MXU_W
            y = y * lax.rsqrt(ss + EPS) * g_ref[:, g0:g0 + MXU_W]
            for j in range(MXU_W // LANES):
                blk = y[:, j * LANES:(j + 1) * LANES]
                yield (b * MXU_W) // LANES + j, rotate(blk) if rotary else blk

    def store_transposed(ref, pair, blk):
        t = blk.T
        for wblk in range(blk.shape[0] // KV_BLK):
            piece = t[:, wblk * KV_BLK:(wblk + 1) * KV_BLK].reshape(2, HEAD_DIM, KV_BLK)
            ref[wblk, 2 * pair:2 * pair + 2] = piece.astype(ref.dtype)

    u = proj(_U0, 2 * CONV_CH)
    u_ref[...] = (u[:, 0:CONV_CH] * jax.nn.sigmoid(u[:, CONV_CH:2 * CONV_CH])).astype(u_ref.dtype)
    q_blocks = ATTN_W // LANES
    for j, blk in head_norm(_QA0, ATTN_W + KV_W, 0, rope):
        if j < q_blocks:
            qa_ref[:, j * LANES:(j + 1) * LANES] = blk.astype(qa_ref.dtype)
        else:
            store_transposed(ka_ref, 0, blk)
    store_transposed(va_ref, 0, proj(_VA0, KV_W))
    for j, blk in head_norm(_QN0, 2 * ATTN_W, ATTN_W + KV_W, False):
        if j < q_blocks:
            qn_ref[:, j * LANES:(j + 1) * LANES] = blk.astype(qn_ref.dtype)
        else:
            store_transposed(kn_ref, j - q_blocks, blk)
    vn = proj(_VN0, ATTN_W)
    for j in range(ATTN_W // LANES):
        store_transposed(vn_ref, j, vn[:, j * LANES:(j + 1) * LANES])


def _in_proj(x, mod4, layer, norm1_g, w_in, gains, rope_tabs, prev_ctx, *, sample):
    tokens = x.shape[0]
    tm = TM
    blocks = tm // KV_BLK
    row = _row_fn(sample, tm)
    in_specs = [
        pl.BlockSpec((tm, D_MODEL), lambda i: (i, 0)),
        _mod_spec(layer, row, 0),
        _mod_spec(layer, row, 1),
        pl.BlockSpec((None, 1, D_MODEL), lambda i: (layer, 0, 0)),
        pl.BlockSpec((None, D_MODEL, IN_W), lambda i: (layer, 0, 0)),
        pl.BlockSpec((None, 1, _GAIN_W), lambda i: (layer, 0, 0)),
    ]
    args = [x, mod4, mod4, norm1_g, w_in, gains]
    aliases = {}
    kv_heads = (GQA_KV_HEADS, GQA_KV_HEADS, NA_HEADS, NA_HEADS)
    if sample:
        tiles_per_seq = DEC_SEQ // tm
        in_specs += [pl.BlockSpec((tm, LANES), lambda i: (i % tiles_per_seq, 0))] * 3
        args += list(rope_tabs)
        kv_specs = [pl.BlockSpec((blocks, nh, HEAD_DIM, KV_BLK), lambda i: (i, 0, 0, 0)) for nh in kv_heads]
        kv_shapes = [jax.ShapeDtypeStruct((tokens // KV_BLK, nh, HEAD_DIM, KV_BLK), _BF16) for nh in kv_heads]
    else:
        kv_specs = [pl.BlockSpec((blocks, None, nh, HEAD_DIM, KV_BLK), lambda i: (i, layer, 0, 0, 0))
                    for nh in kv_heads]
        kv_shapes = [jax.ShapeDtypeStruct((BATCH, DEPTH, nh, HEAD_DIM, SEQ), _F32) for nh in kv_heads]
        if prev_ctx is not None:
            kv_out_index = (2, 3, 5, 6)
            for arr, out_idx in zip(prev_ctx, kv_out_index):
                aliases[len(args)] = out_idx
                in_specs.append(pl.BlockSpec(memory_space=pl.ANY))
                args.append(arr)

    def tok(w):
        return pl.BlockSpec((tm, w), lambda i: (i, 0))

    def tok_shape(w):
        return jax.ShapeDtypeStruct((tokens, w), _BF16)

    out_specs = [tok(CONV_CH), tok(ATTN_W), kv_specs[0], kv_specs[1], tok(ATTN_W), kv_specs[2], kv_specs[3]]
    out_shape = [tok_shape(CONV_CH), tok_shape(ATTN_W), kv_shapes[0], kv_shapes[1],
                 tok_shape(ATTN_W), kv_shapes[2], kv_shapes[3]]
    return pl.pallas_call(
        functools.partial(_in_proj_kernel, rope=sample, n_alias=len(aliases)),
        grid=(tokens // tm,),
        in_specs=in_specs,
        out_specs=out_specs,
        out_shape=out_shape,
        input_output_aliases=aliases,
        compiler_params=_params(("arbitrary",), 48),
        name="in_proj_sample" if sample else "in_proj_prompt",
    )(*args)


_CONV_ROWS = 64
_CONV_PAD = 16


def _conv_kernel(h_ref, w_ref, b_ref, g_ref, beta_ref, o_ref, hp_ref, *, seq):
    zeros = jnp.zeros((_CONV_PAD, CONV_CH), _F32)
    hp_ref[0:_CONV_PAD, :] = zeros
    hp_ref[_CONV_PAD + seq:2 * _CONV_PAD + seq, :] = zeros
    copy_rows = 256

    def copy_body(i, carry):
        t0 = pl.multiple_of(i * copy_rows, copy_rows)
        hp_ref[pl.ds(_CONV_PAD + t0, copy_rows), :] = h_ref[pl.ds(t0, copy_rows), :].astype(_F32)
        return carry

    lax.fori_loop(0, seq // copy_rows, copy_body, 0)

    first = _CONV_PAD - CONV_K // 2
    span = _CONV_ROWS + SUBLANES

    def conv_body(i, carry):
        t0 = pl.multiple_of(i * _CONV_ROWS, _CONV_ROWS)
        halves = []
        for c0 in range(0, CONV_CH, LANES):
            lanes = slice(c0, c0 + LANES)
            acc = jnp.zeros((_CONV_ROWS, LANES), _F32) + b_ref[:, lanes]
            for s in range(SUBLANES):
                part = None
                for a in range((first + CONV_K + SUBLANES - 1) // SUBLANES):
                    k = SUBLANES * a + s - first
                    if 0 <= k < CONV_K:
                        term = hp_ref[pl.ds(t0 + SUBLANES * a, span), lanes] * w_ref[k:k + 1, lanes]
                        part = term if part is None else part + term
                acc = acc + part[s:s + _CONV_ROWS, :]
            halves.append(acc)
        acc = jnp.concatenate(halves, axis=1)
        mu = jnp.mean(acc, axis=-1, keepdims=True)
        cen = acc - mu
        var = jnp.mean(cen * cen, axis=-1, keepdims=True)
        y = cen * lax.rsqrt(var + EPS) * g_ref[...] + beta_ref[...]
        o_ref[pl.ds(t0, _CONV_ROWS), :] = (y * jax.nn.sigmoid(y)).astype(o_ref.dtype)
        return carry

    lax.fori_loop(0, seq // _CONV_ROWS, conv_body, 0, unroll=2)


def _conv(u, layer, dw_w, dw_b, ln_g, ln_b, *, seq):
    nb = u.shape[0] // seq
    u3 = u.reshape(nb, seq, CONV_CH)
    vec = pl.BlockSpec((None, 1, CONV_CH), lambda b: (layer, 0, 0))
    out = pl.pallas_call(
        functools.partial(_conv_kernel, seq=seq),
        grid=(nb,),
        in_specs=[
            pl.BlockSpec((None, seq, CONV_CH), lambda b: (b, 0, 0)),
            pl.BlockSpec((None, CONV_K, CONV_CH), lambda b: (layer, 0, 0)),
            vec, vec, vec,
        ],
        out_specs=pl.BlockSpec((None, seq, CONV_CH), lambda b: (b, 0, 0)),
        out_shape=jax.ShapeDtypeStruct((nb, seq, CONV_CH), _BF16),
        scratch_shapes=[pltpu.VMEM((seq + 2 * _CONV_PAD, CONV_CH), _F32)],
        compiler_params=_params(("arbitrary",), 32),
        name="conv_module",
    )(u3, dw_w, dw_b, ln_g, ln_b)
    return out.reshape(nb * seq, CONV_CH)


def _softmax_pv_t(s, value_t_blocks, key_splits):
    m = jnp.max(s, axis=-1, keepdims=True)
    e = jnp.exp2(s - m)
    denom = jnp.sum(e, axis=-1, keepdims=True)
    eb = e.astype(_BF16)
    out = None
    for i, vt in enumerate(value_t_blocks):
        pv = _dot_nt(eb[:, key_splits[i]:key_splits[i + 1]], vt)
        out = pv if out is None else out + pv
    return out / denom


def _head_rows(block, parity):
    rows = lax.broadcasted_iota(jnp.int32, block.shape, 0)
    keep = rows < HEAD_DIM if parity == 0 else rows >= HEAD_DIM
    return jnp.where(keep, block, jnp.zeros_like(block))


def _pair_block(ref, idx, pair):
    return ref[idx + (slice(2 * pair, 2 * pair + 2),)].reshape(PAIR, KV_BLK)


def _prompt_attn_kernel(qa_ref, ka_ref, va_ref, qn_ref, kn_ref, vn_ref, b_ref, c_ref):
    low_out = _low_half((SEQ, LANES))
    zero = jnp.zeros((HEAD_DIM, SEQ), _BF16)
    for p in range(GQA_HEADS // 2):
        cols = slice(p * LANES, (p + 1) * LANES)
        q = qa_ref[:, cols]
        outs = []
        for parity in range(2):
            kv = (2 * p + parity) // GQA_GROUP
            k = ka_ref[kv].astype(_BF16)
            v = va_ref[kv].astype(_BF16)
            k_sel = jnp.concatenate([k, zero] if parity == 0 else [zero, k], axis=0)
            outs.append(_softmax_pv_t(_dot(q, k_sel), [jnp.concatenate([v, v], axis=0)], (0, SEQ)))
        b_ref[:, cols] = jnp.where(low_out, outs[0], outs[1]).astype(b_ref.dtype)
    for p in range(NA_HEADS // 2):
        cols = slice(p * LANES, (p + 1) * LANES)
        q = qn_ref[:, cols]
        k = _pair_block(kn_ref, (), p).astype(_BF16)
        v = _pair_block(vn_ref, (), p).astype(_BF16)
        outs = [_softmax_pv_t(_dot(q, _head_rows(k, parity)), [v], (0, SEQ)) for parity in range(2)]
        c_ref[:, cols] = jnp.where(low_out, outs[0], outs[1]).astype(c_ref.dtype)


def _prompt_attention(qa, ka_t, va_t, qn, kn_t, vn_t, layer):
    def tok(w):
        return pl.BlockSpec((SEQ, w), lambda b: (b, 0))

    def ctx(nh):
        return pl.BlockSpec((None, None, nh, HEAD_DIM, SEQ), lambda b: (b, layer, 0, 0, 0))

    return pl.pallas_call(
        _prompt_attn_kernel,
        grid=(BATCH,),
        in_specs=[tok(ATTN_W), ctx(GQA_KV_HEADS), ctx(GQA_KV_HEADS), tok(ATTN_W), ctx(NA_HEADS), ctx(NA_HEADS)],
        out_specs=[tok(ATTN_W), tok(ATTN_W)],
        out_shape=[jax.ShapeDtypeStruct((BATCH * SEQ, ATTN_W), _BF16)] * 2,
        compiler_params=_params(("arbitrary",), 32),
        name="prompt_attention",
    )(qa, ka_t, va_t, qn, kn_t, vn_t)


_GQA_TQ = 256
_GQA_KEYS = DEC_SEQ + PAST_LEN


def _gqa_kernel(q_ref, k_ref, v_ref, kc_ref, vc_ref, o_ref, ksel_ref, vdup_ref, s0_ref, s1_ref):
    @pl.when(pl.program_id(1) == 0)
    def _():
        ksel_ref[...] = jnp.zeros(ksel_ref.shape, _BF16)
        for blk in range(KV_BLOCKS + 1):
            lanes = slice(blk * KV_BLK, (blk + 1) * KV_BLK)
            for kv in range(GQA_KV_HEADS):
                if blk < KV_BLOCKS:
                    k, v = k_ref[blk, kv], v_ref[blk, kv]
                else:
                    k, v = kc_ref[kv].astype(_BF16), vc_ref[kv].astype(_BF16)
                ksel_ref[2 * kv, 0:HEAD_DIM, lanes] = k
                ksel_ref[2 * kv + 1, HEAD_DIM:PAIR, lanes] = k
                vdup_ref[kv, 0:HEAD_DIM, lanes] = v
                vdup_ref[kv, HEAD_DIM:PAIR, lanes] = v

    low_out = _low_half((_GQA_TQ, LANES))
    s_refs = (s0_ref, s1_ref)

    def scores(head):
        p = head // 2
        kid = 2 * (head // GQA_GROUP) + head % 2
        s_refs[head % 2][...] = _dot(q_ref[:, p * LANES:(p + 1) * LANES], ksel_ref[kid])

    scores(0)
    outs = []
    for head in range(GQA_HEADS):
        if head + 1 < GQA_HEADS:
            scores(head + 1)
        outs.append(_softmax_pv_t(s_refs[head % 2][...], [vdup_ref[head // GQA_GROUP]], (0, _GQA_KEYS)))
        if head % 2 == 1:
            p = head // 2
            o_ref[:, p * LANES:(p + 1) * LANES] = jnp.where(low_out, outs[0], outs[1]).astype(o_ref.dtype)
            outs = []


def _sample_gqa(q, k_t, v_t, cache_k_t, cache_v_t, layer):
    tiles = DEC_SEQ // _GQA_TQ
    new_spec = pl.BlockSpec((KV_BLOCKS, GQA_KV_HEADS, HEAD_DIM, KV_BLK), lambda b, i: (b, 0, 0, 0))
    cache_spec = pl.BlockSpec((None, None, GQA_KV_HEADS, HEAD_DIM, PAST_LEN), lambda b, i: (b, layer, 0, 0, 0))
    return pl.pallas_call(
        _gqa_kernel,
        grid=(DEC_BATCH, tiles),
        in_specs=[pl.BlockSpec((_GQA_TQ, ATTN_W), lambda b, i: (b * tiles + i, 0)),
                  new_spec, new_spec, cache_spec, cache_spec],
        out_specs=pl.BlockSpec((_GQA_TQ, ATTN_W), lambda b, i: (b * tiles + i, 0)),
        out_shape=jax.ShapeDtypeStruct((DEC_BATCH * DEC_SEQ, ATTN_W), _BF16),
        scratch_shapes=[pltpu.VMEM((4, PAIR, _GQA_KEYS), _BF16), pltpu.VMEM((2, PAIR, _GQA_KEYS), _BF16),
                        pltpu.VMEM((_GQA_TQ, _GQA_KEYS), _F32), pltpu.VMEM((_GQA_TQ, _GQA_KEYS), _F32)],
        compiler_params=_params(("arbitrary", "arbitrary"), 48),
        name="sample_gqa",
    )(q, k_t, v_t, cache_k_t, cache_v_t)


_NA_GROUP = 4
_NA_GROUPS = GRID_H // _NA_GROUP
_NA_GROUPS_PER_STEP = 2
_NA_WIN_ROWS = 12
_NA_Q = _NA_GROUP * GRID_W
_NA_KEYS = _NA_WIN_ROWS * GRID_W
_NA_WIN_BLOCKS = _NA_KEYS // KV_BLK
_NA_VARIANTS = 3
_NA_OFFSETS = 2 * NA_WIN_R - 1
_NA_RPB_ROWS = 16

assert _NA_GROUP * GRID_W == KV_BLK


def _na_window_start(group):
    lo, hi = 0, GRID_H - _NA_WIN_ROWS
    start = _NA_GROUP * group - NA_WIN_R // 2
    if isinstance(group, int):
        return min(max(start, lo), hi)
    return jnp.clip(start, lo, hi)


def _na_row_plan():
    plan = []
    for group in (0, 1, _NA_GROUPS - 1):
        w0 = _na_window_start(group)
        rows = []
        for i in range(_NA_GROUP):
            r = _NA_GROUP * group + i
            rs = min(max(r - NA_WIN_R // 2, 0), GRID_H - NA_WIN_R)
            rows.append([kr - r + NA_WIN_R - 1 if rs <= kr < rs + NA_WIN_R else None
                         for kr in range(w0, w0 + _NA_WIN_ROWS)])
        plan.append(rows)
    return plan


def _na_build_bias(rpb_ref, bias_ref, t2_ref):
    shape = (GRID_W, LANES)
    c = lax.broadcasted_iota(jnp.int32, shape, 0)
    lane = lax.broadcasted_iota(jnp.int32, shape, 1)
    kc = lane % GRID_W
    cs = jnp.clip(c - NA_WIN_C // 2, 0, GRID_W - NA_WIN_C)
    col_ok = (kc >= cs) & (kc < cs + NA_WIN_C)
    low = lane < GRID_W
    neg = jnp.full(shape, NEG_INF, _F32)
    plan = _na_row_plan()

    def head_body(h, carry):
        for d in range(_NA_OFFSETS):
            v = jnp.broadcast_to(rpb_ref[h, d:d + 1, :], shape)
            lo = pltpu.roll(v, LANES - (NA_WIN_C - 1), 1, stride=1, stride_axis=0)
            hi = pltpu.roll(v, GRID_W - (NA_WIN_C - 1), 1, stride=1, stride_axis=0)
            t2_ref[d] = jnp.where(col_ok, jnp.where(low, lo, hi) * LOG2E, neg)
        for variant in range(_NA_VARIANTS):
            for i in range(_NA_GROUP):
                for m in range(_NA_WIN_ROWS // 2):
                    d_even, d_odd = plan[variant][i][2 * m], plan[variant][i][2 * m + 1]
                    even = neg if d_even is None else t2_ref[d_even]
                    odd = neg if d_odd is None else t2_ref[d_odd]
                    tile = neg if d_even is None and d_odd is None else jnp.where(low, even, odd)
                    bias_ref[variant, h, i * GRID_W:(i + 1) * GRID_W, m * LANES:(m + 1) * LANES] = tile
        return carry

    lax.fori_loop(0, NA_HEADS, head_body, 0)


def _na_kernel(q_ref, k_ref, v_ref, kc_ref, vc_ref, rpb_ref, o_ref, bias_ref, t2_ref, s0_ref, s1_ref):
    step = pl.program_id(1)

    @pl.when((pl.program_id(0) == 0) & (step == 0))
    def _():
        _na_build_bias(rpb_ref, bias_ref, t2_ref)

    low_out = _low_half((_NA_Q, LANES))
    s_refs = (s0_ref, s1_ref)
    units = [(g, h) for g in range(_NA_GROUPS_PER_STEP) for h in range(NA_HEADS)]
    splits = tuple(range(0, _NA_KEYS + PAST_LEN + 1, KV_BLK))

    def geometry(g):
        group = step * _NA_GROUPS_PER_STEP + g
        variant = jnp.where(group == 0, 0, jnp.where(group == _NA_GROUPS - 1, 2, 1))
        return variant, _na_window_start(group) // _NA_GROUP

    geo = [geometry(g) for g in range(_NA_GROUPS_PER_STEP)]

    def scores(u):
        g, h = units[u]
        variant, blk0 = geo[g]
        p, parity = divmod(h, 2)
        q = q_ref[g * _NA_Q:(g + 1) * _NA_Q, p * LANES:(p + 1) * LANES]
        s_ref = s_refs[u % 2]
        for w in range(_NA_WIN_BLOCKS):
            lanes = slice(w * KV_BLK, (w + 1) * KV_BLK)
            k = _head_rows(_pair_block(k_ref, (blk0 + w,), p), parity)
            s_ref[:, lanes] = _dot(q, k) + bias_ref[variant, h, :, lanes]
        kc = _head_rows(_pair_block(kc_ref, (), p).astype(_BF16), parity)
        s_ref[:, _NA_KEYS:_NA_KEYS + PAST_LEN] = _dot(q, kc)

    scores(0)
    outs = []
    for u, (g, h) in enumerate(units):
        if u + 1 < len(units):
            scores(u + 1)
        _, blk0 = geo[g]
        p = h // 2
        values = [_pair_block(v_ref, (blk0 + w,), p) for w in range(_NA_WIN_BLOCKS)]
        values.append(_pair_block(vc_ref, (), p).astype(_BF16))
        outs.append(_softmax_pv_t(s_refs[u % 2][...], values, splits))
        if h % 2 == 1:
            o_ref[g * _NA_Q:(g + 1) * _NA_Q, p * LANES:(p + 1) * LANES] = (
                jnp.where(low_out, outs[0], outs[1]).astype(o_ref.dtype))
            outs = []


def _sample_na(q, k_t, v_t, cache_k_t, cache_v_t, rpb, layer):
    steps = _NA_GROUPS // _NA_GROUPS_PER_STEP
    tq = _NA_GROUPS_PER_STEP * _NA_Q
    new_spec = pl.BlockSpec((KV_BLOCKS, NA_HEADS, HEAD_DIM, KV_BLK), lambda b, i: (b, 0, 0, 0))
    cache_spec = pl.BlockSpec((None, None, NA_HEADS, HEAD_DIM, PAST_LEN), lambda b, i: (b, layer, 0, 0, 0))
    return pl.pallas_call(
        _na_kernel,
        grid=(DEC_BATCH, steps),
        in_specs=[pl.BlockSpec((tq, ATTN_W), lambda b, i: (b * steps + i, 0)),
                  new_spec, new_spec, cache_spec, cache_spec,
                  pl.BlockSpec((None, NA_HEADS, _NA_RPB_ROWS, LANES), lambda b, i: (layer, 0, 0, 0))],
        out_specs=pl.BlockSpec((tq, ATTN_W), lambda b, i: (b * steps + i, 0)),
        out_shape=jax.ShapeDtypeStruct((DEC_BATCH * DEC_SEQ, ATTN_W), _BF16),
        scratch_shapes=[pltpu.VMEM((_NA_VARIANTS, NA_HEADS, _NA_Q, _NA_KEYS), _F32),
                        pltpu.VMEM((_NA_OFFSETS, GRID_W, LANES), _F32),
                        pltpu.VMEM((_NA_Q, _NA_KEYS + PAST_LEN), _F32),
                        pltpu.VMEM((_NA_Q, _NA_KEYS + PAST_LEN), _F32)],
        compiler_params=_params(("arbitrary", "arbitrary"), 48),
        name="sample_na",
    )(q, k_t, v_t, cache_k_t, cache_v_t, rpb)


def _out_mlp_kernel(a_ref, b_ref, c_ref, x_ref, g1_ref, sh2_ref, sc2_ref, g2_ref, n2_ref,
                    wo_ref, w1_ref, w2_ref, o_ref):
    b0 = CONV_CH
    c0 = CONV_CH + ATTN_W
    y = (_dot(a_ref[...], wo_ref[0:b0, :]) + _dot(b_ref[...], wo_ref[b0:c0, :])
         + _dot(c_ref[...], wo_ref[c0:D_MODEL, :]))
    x1 = x_ref[...] + g1_ref[...] * y
    ms = jnp.mean(x1 * x1, axis=-1, keepdims=True)
    h = (x1 * lax.rsqrt(ms + EPS)) * n2_ref[...]
    hb = (h * (1.0 + sc2_ref[...]) + sh2_ref[...]).astype(_BF16)
    acc = jnp.zeros(x1.shape, _F32)
    for j in range(D_FF // FF_CHUNK):
        t = jnp.maximum(_dot(hb, w1_ref[:, j * FF_CHUNK:(j + 1) * FF_CHUNK]), 0.0)
        acc = acc + _dot((t * t).astype(_BF16), w2_ref[j * FF_CHUNK:(j + 1) * FF_CHUNK, :])
    o_ref[...] = x1 + g2_ref[...] * acc


def _out_mlp(a, b, c, x, mod4, layer, norm2_g, w_out, w1, w2, *, sample):
    tokens = x.shape[0]
    tm = TM
    row = _row_fn(sample, tm)

    def tok(w):
        return pl.BlockSpec((tm, w), lambda i: (i, 0))

    return pl.pallas_call(
        _out_mlp_kernel,
        grid=(tokens // tm,),
        in_specs=[tok(CONV_CH), tok(ATTN_W), tok(ATTN_W), tok(D_MODEL),
                  _mod_spec(layer, row, 2), _mod_spec(layer, row, 3),
                  _mod_spec(layer, row, 4), _mod_spec(layer, row, 5),
                  pl.BlockSpec((None, 1, D_MODEL), lambda i: (layer, 0, 0)),
                  pl.BlockSpec((None, D_MODEL, D_MODEL), lambda i: (layer, 0, 0)),
                  pl.BlockSpec((None, D_MODEL, D_FF), lambda i: (layer, 0, 0)),
                  pl.BlockSpec((None, D_FF, D_MODEL), lambda i: (layer, 0, 0))],
        out_specs=tok(D_MODEL),
        out_shape=jax.ShapeDtypeStruct((tokens, D_MODEL), _F32),
        compiler_params=_params(("arbitrary",), 56),
        name="out_mlp",
    )(a, b, c, x, mod4, mod4, mod4, mod4, norm2_g, w_out, w1, w2)


def kernel(x_prompt, x_sample, cache_attn_k, cache_attn_v, cache_na_k, cache_na_v, c, c_ctx, ada_w, ada_b, norm1_g, norm2_g, w_in, conv_dw_w, conv_dw_b, conv_ln_g, conv_ln_b, attn_q_g, attn_k_g, na_q_g, na_k_g, na_rpb, w_out, mlp_w1, mlp_w2):
    cvec = jnp.concatenate(
        [c_ctx[None, :], c, jnp.zeros((MOD_ROWS - 1 - DEC_BATCH, D_MODEL), _F32)], axis=0)
    mod4 = _modulation(cvec, ada_w, ada_b).reshape(DEPTH, MOD_ROWS, 1, 6 * D_MODEL)

    def heads(g, n):
        return jnp.tile(g, (1, n))
    gains = jnp.concatenate(
        [heads(attn_q_g, GQA_HEADS) * Q_SCALE, heads(attn_k_g, GQA_KV_HEADS),
         heads(na_q_g, NA_HEADS) * Q_SCALE, heads(na_k_g, NA_HEADS)], axis=1)[:, None, :]
    rope_tabs = _rope_tables()
    rpb = jnp.pad(na_rpb, ((0, 0), (0, 0), (0, _NA_RPB_ROWS - _NA_OFFSETS),
                           (0, LANES - (2 * NA_WIN_C - 1))))

    w_in_b = w_in.astype(_BF16)
    w_out_b = w_out.astype(_BF16)
    w1_b = mlp_w1.astype(_BF16)
    w2_b = mlp_w2.astype(_BF16)
    n1 = norm1_g.reshape(DEPTH, 1, D_MODEL)
    n2 = norm2_g.reshape(DEPTH, 1, D_MODEL)
    dw_b = conv_dw_b.reshape(DEPTH, 1, CONV_CH)
    ln_g = conv_ln_g.reshape(DEPTH, 1, CONV_CH)
    ln_b = conv_ln_b.reshape(DEPTH, 1, CONV_CH)
    to_t = (0, 1, 3, 4, 2)
    ck_a, cv_a, ck_n, cv_n = (t.transpose(to_t) for t in (cache_attn_k, cache_attn_v, cache_na_k, cache_na_v))

    xp = x_prompt.reshape(BATCH * SEQ, D_MODEL)
    xs = x_sample.reshape(DEC_BATCH * DEC_SEQ, D_MODEL)
    ctx = None
    for layer in range(DEPTH):
        u, qa, ka, va, qn, kn, vn = _in_proj(xp, mod4, layer, n1, w_in_b, gains, None, ctx, sample=False)
        ctx = (ka, va, kn, vn)
        a_out = _conv(u, layer, conv_dw_w, dw_b, ln_g, ln_b, seq=SEQ)
        b_out, c_out = _prompt_attention(qa, ka, va, qn, kn, vn, layer)
        xp = _out_mlp(a_out, b_out, c_out, xp, mod4, layer, n2, w_out_b, w1_b, w2_b, sample=False)
        u, qa, ka, va, qn, kn, vn = _in_proj(xs, mod4, layer, n1, w_in_b, gains, rope_tabs, None, sample=True)
        a_out = _conv(u, layer, conv_dw_w, dw_b, ln_g, ln_b, seq=DEC_SEQ)
        b_out = _sample_gqa(qa, ka, va, ck_a, cv_a, layer)
        c_out = _sample_na(qn, kn, vn, ck_n, cv_n, rpb, layer)
        xs = _out_mlp(a_out, b_out, c_out, xs, mod4, layer, n2, w_out_b, w1_b, w2_b, sample=True)

    outs = [t.transpose(0, 1, 4, 2, 3) for t in ctx]
    return (xp.reshape(BATCH, SEQ, D_MODEL), xs.reshape(DEC_BATCH, DEC_SEQ, D_MODEL), *outs)
```

```python
import functools
import math

import numpy as np
import jax
import jax.numpy as jnp
from jax import lax
from jax.experimental import pallas as pl
from jax.experimental.pallas import tpu as pltpu

D_MODEL = 1024
BATCH = 32
SEQ = 256
DEPTH = 4
DEC_BATCH = 8
DEC_SEQ = 2048
PAST_LEN = 256
GRID_W = 64
GRID_H = DEC_SEQ // GRID_W
HEAD_DIM = 64
CONV_CH = D_MODEL // 4
CONV_K = 31
ATTN_W = (D_MODEL - CONV_CH) // 2
GQA_HEADS = ATTN_W // HEAD_DIM
GQA_KV_HEADS = GQA_HEADS // 3
GQA_GROUP = GQA_HEADS // GQA_KV_HEADS
NA_HEADS = ATTN_W // HEAD_DIM
D_FF = 4 * D_MODEL
NA_WIN_R = 8
NA_WIN_C = 16
ROPE_THETA = 10000.0
EPS = 1e-6
NEG_INF = -1e30

KV_W = GQA_KV_HEADS * HEAD_DIM
IN_W = 2 * CONV_CH + ATTN_W + 2 * KV_W + 3 * ATTN_W
_U0, _QA0, _KA0, _VA0, _QN0, _KN0, _VN0 = 0, 512, 896, 1024, 1152, 1536, 1920
LANES = 128
SUBLANES = 8
MXU_W = 256
MOD_ROWS = 16
LOG2E = math.log2(math.e)
Q_SCALE = HEAD_DIM ** -0.5 * LOG2E
TM = 512
FF_CHUNK = 512
KV_BLK = 256
KV_BLOCKS = DEC_SEQ // KV_BLK
PAIR = 2 * HEAD_DIM
MIB = 1024 * 1024

assert SEQ == KV_BLK and PAST_LEN == KV_BLK and TM % KV_BLK == 0 and PAIR == LANES

_F32 = jnp.float32
_BF16 = jnp.bfloat16


def _params(semantics, vmem_mib):
    return pltpu.CompilerParams(dimension_semantics=semantics, vmem_limit_bytes=vmem_mib * MIB)


def _dot(a, b):
    return jnp.dot(a, b, preferred_element_type=_F32)


def _dot_nt(a, b):
    return lax.dot_general(a, b, (((1,), (1,)), ((), ())), preferred_element_type=_F32)


def _low_half(shape):
    return lax.broadcasted_iota(jnp.int32, shape, len(shape) - 1) % LANES < HEAD_DIM


def _mod_kernel(c_ref, w_ref, b_ref, o_ref):
    c = c_ref[...]
    s = (c * jax.nn.sigmoid(c)).astype(_BF16)
    o_ref[...] = _dot(s, w_ref[...].astype(_BF16)) + b_ref[...]


def _modulation(cvec, ada_w, ada_b):
    tn = 1536
    return pl.pallas_call(
        _mod_kernel,
        grid=(DEPTH, 6 * D_MODEL // tn),
        in_specs=[
            pl.BlockSpec((MOD_ROWS, D_MODEL), lambda l, j: (0, 0)),
            pl.BlockSpec((None, D_MODEL, tn), lambda l, j: (l, 0, j)),
            pl.BlockSpec((None, 1, tn), lambda l, j: (l, 0, j)),
        ],
        out_specs=pl.BlockSpec((None, MOD_ROWS, tn), lambda l, j: (l, 0, j)),
        out_shape=jax.ShapeDtypeStruct((DEPTH, MOD_ROWS, 6 * D_MODEL), _F32),
        compiler_params=_params(("arbitrary", "arbitrary"), 40),
        name="modulation",
    )(cvec, ada_w, ada_b.reshape(DEPTH, 1, 6 * D_MODEL))


def _mod_spec(layer, row_fn, part):
    return pl.BlockSpec((None, None, 1, D_MODEL), lambda i: (layer, row_fn(i), 0, part))


def _row_fn(sample, tm):
    tiles_per_seq = DEC_SEQ // tm
    if sample:
        return lambda i: 1 + i // tiles_per_seq
    return lambda i: 0


def _rope_tables():
    t = np.arange(DEC_SEQ)
    pos = (t // GRID_W, t % GRID_W)
    half = HEAD_DIM // 2
    quarter = half // 2
    inv = 1.0 / (ROPE_THETA ** (np.arange(quarter) * 2.0 / half))
    lane = np.arange(LANES)
    d = lane % HEAD_DIM
    axis = d // half
    second = (d % half) >= quarter
    p = np.where(axis[None, :] == 0, pos[0][:, None], pos[1][:, None]).astype(np.float64)
    ang = p * inv[d % quarter][None, :]
    cos, sin = np.cos(ang), np.sin(ang)
    sin_next = np.where(second[None, :], 0.0, -sin)
    sin_prev = np.where(second[None, :], sin, 0.0)
    return tuple(jnp.asarray(a, _F32) for a in (cos, sin_next, sin_prev))


_GAIN_W = 2 * ATTN_W + KV_W + ATTN_W
_SUB_TILE = 512
_IN_PROJ_TM = 2 * _SUB_TILE

assert _SUB_TILE % KV_BLK == 0 and DEC_SEQ % _IN_PROJ_TM == 0


def _in_proj_kernel(*refs, rope, n_alias):
    n_in = 9 if rope else 6
    if rope:
        x_ref, sh_ref, sc_ref, n1_ref, w_ref, g_ref, cos_ref, sn_ref, sp_ref = refs[:n_in]
    else:
        x_ref, sh_ref, sc_ref, n1_ref, w_ref, g_ref = refs[:n_in]
    u_ref, qa_ref, ka_ref, va_ref, qn_ref, kn_ref, vn_ref = refs[n_in + n_alias:]

    r = lax.broadcasted_iota(jnp.int32, (MXU_W, MXU_W), 0) // HEAD_DIM
    c = lax.broadcasted_iota(jnp.int32, (MXU_W, MXU_W), 1) // HEAD_DIM
    seg_mean = jnp.where(r == c, 1.0 / HEAD_DIM, 0.0).astype(_BF16)

    def normed_input(rows):
        x = x_ref[rows, :]
        ms = jnp.mean(x * x, axis=-1, keepdims=True)
        h = (x * lax.rsqrt(ms + EPS)) * n1_ref[...]
        return (h * (1.0 + sc_ref[...]) + sh_ref[...]).astype(_BF16)

    def project(sub, hb):
        rows = slice(sub * _SUB_TILE, (sub + 1) * _SUB_TILE)
        blk0 = sub * (_SUB_TILE // KV_BLK)

        y_all = _dot(hb, w_ref[...])

        def proj(c0, width):
            return y_all[:, c0:c0 + width]

        def rotate(blk):
            return (blk * cos_ref[rows, :] + pltpu.roll(blk, LANES - 16, 1) * sn_ref[rows, :]
                    + pltpu.roll(blk, 16, 1) * sp_ref[rows, :])

        def head_norm(c0, width, gain0, rotary):
            for b in range(width // MXU_W):
                y = proj(c0 + b * MXU_W, MXU_W)
                ss = _dot((y * y).astype(_BF16), seg_mean)
                g0 = gain0 + b * MXU_W
                y = y * lax.rsqrt(ss + EPS) * g_ref[:, g0:g0 + MXU_W]
                for j in range(MXU_W // LANES):
                    blk = y[:, j * LANES:(j + 1) * LANES]
                    yield (b * MXU_W) // LANES + j, rotate(blk) if rotary else blk

        def store_transposed(ref, pair, blk):
            t = blk.T
            for wblk in range(_SUB_TILE // KV_BLK):
                piece = t[:, wblk * KV_BLK:(wblk + 1) * KV_BLK].reshape(2, HEAD_DIM, KV_BLK)
                ref[blk0 + wblk, 2 * pair:2 * pair + 2] = piece.astype(ref.dtype)

        u = proj(_U0, 2 * CONV_CH)
        u_ref[rows, :] = (u[:, 0:CONV_CH] * jax.nn.sigmoid(u[:, CONV_CH:2 * CONV_CH])).astype(u_ref.dtype)
        q_blocks = ATTN_W // LANES
        for j, blk in head_norm(_QA0, ATTN_W + KV_W, 0, rope):
            if j < q_blocks:
                qa_ref[rows, j * LANES:(j + 1) * LANES] = blk.astype(qa_ref.dtype)
            else:
                store_transposed(ka_ref, 0, blk)
        store_transposed(va_ref, 0, proj(_VA0, KV_W))
        for j, blk in head_norm(_QN0, 2 * ATTN_W, ATTN_W + KV_W, False):
            if j < q_blocks:
                qn_ref[rows, j * LANES:(j + 1) * LANES] = blk.astype(qn_ref.dtype)
            else:
                store_transposed(kn_ref, j - q_blocks, blk)
        vn = proj(_VN0, ATTN_W)
        for j in range(ATTN_W // LANES):
            store_transposed(vn_ref, j, vn[:, j * LANES:(j + 1) * LANES])

    n_sub = x_ref.shape[0] // _SUB_TILE
    inputs = [normed_input(slice(s * _SUB_TILE, (s + 1) * _SUB_TILE)) for s in range(n_sub)]
    for s in range(n_sub):
        project(s, inputs[s])


def _in_proj(x, mod4, layer, norm1_g, w_in, gains, rope_tabs, prev_ctx, *, sample):
    tokens = x.shape[0]
    tm = _IN_PROJ_TM
    blocks = tm // KV_BLK
    row = _row_fn(sample, tm)
    in_specs = [
        pl.BlockSpec((tm, D_MODEL), lambda i: (i, 0)),
        _mod_spec(layer, row, 0),
        _mod_spec(layer, row, 1),
        pl.BlockSpec((None, 1, D_MODEL), lambda i: (layer, 0, 0)),
        pl.BlockSpec((None, D_MODEL, IN_W), lambda i: (layer, 0, 0), pipeline_mode=pl.Buffered(1)),
        pl.BlockSpec((None, 1, _GAIN_W), lambda i: (layer, 0, 0)),
    ]
    args = [x, mod4, mod4, norm1_g, w_in, gains]
    aliases = {}
    kv_heads = (GQA_KV_HEADS, GQA_KV_HEADS, NA_HEADS, NA_HEADS)
    if sample:
        tiles_per_seq = DEC_SEQ // tm
        in_specs += [pl.BlockSpec((tm, LANES), lambda i: (i % tiles_per_seq, 0))] * 3
        args += list(rope_tabs)
        kv_specs = [pl.BlockSpec((blocks, nh, HEAD_DIM, KV_BLK), lambda i: (i, 0, 0, 0)) for nh in kv_heads]
        kv_shapes = [jax.ShapeDtypeStruct((tokens // KV_BLK, nh, HEAD_DIM, KV_BLK), _BF16) for nh in kv_heads]
    else:
        kv_specs = [pl.BlockSpec((blocks, None, nh, HEAD_DIM, KV_BLK), lambda i: (i, layer, 0, 0, 0))
                    for nh in kv_heads]
        kv_shapes = [jax.ShapeDtypeStruct((BATCH, DEPTH, nh, HEAD_DIM, SEQ), _F32) for nh in kv_heads]
        if prev_ctx is not None:
            kv_out_index = (2, 3, 5, 6)
            for arr, out_idx in zip(prev_ctx, kv_out_index):
                aliases[len(args)] = out_idx
                in_specs.append(pl.BlockSpec(memory_space=pl.ANY))
                args.append(arr)

    def tok(w):
        return pl.BlockSpec((tm, w), lambda i: (i, 0))

    def tok_shape(w):
        return jax.ShapeDtypeStruct((tokens, w), _BF16)

    out_specs = [tok(CONV_CH), tok(ATTN_W), kv_specs[0], kv_specs[1], tok(ATTN_W), kv_specs[2], kv_specs[3]]
    out_shape = [tok_shape(CONV_CH), tok_shape(ATTN_W), kv_shapes[0], kv_shapes[1],
                 tok_shape(ATTN_W), kv_shapes[2], kv_shapes[3]]
    return pl.pallas_call(
        functools.partial(_in_proj_kernel, rope=sample, n_alias=len(aliases)),
        grid=(tokens // tm,),
        in_specs=in_specs,
        out_specs=out_specs,
        out_shape=out_shape,
        input_output_aliases=aliases,
        compiler_params=_params(("arbitrary",), 48),
        name="in_proj_sample" if sample else "in_proj_prompt",
    )(*args)


_CONV_ROWS = 64
_CONV_PAD = 16


def _conv_kernel(h_ref, w_ref, b_ref, g_ref, beta_ref, o_ref, hp_ref, *, seq):
    zeros = jnp.zeros((_CONV_PAD, CONV_CH), _F32)
    hp_ref[0:_CONV_PAD, :] = zeros
    hp_ref[_CONV_PAD + seq:2 * _CONV_PAD + seq, :] = zeros
    copy_rows = 256

    def copy_body(i, carry):
        t0 = pl.multiple_of(i * copy_rows, copy_rows)
        hp_ref[pl.ds(_CONV_PAD + t0, copy_rows), :] = h_ref[pl.ds(t0, copy_rows), :].astype(_F32)
        return carry

    lax.fori_loop(0, seq // copy_rows, copy_body, 0)

    first = _CONV_PAD - CONV_K // 2
    span = _CONV_ROWS + SUBLANES

    def conv_body(i, carry):
        t0 = pl.multiple_of(i * _CONV_ROWS, _CONV_ROWS)
        halves = []
        for c0 in range(0, CONV_CH, LANES):
            lanes = slice(c0, c0 + LANES)
            acc = jnp.zeros((_CONV_ROWS, LANES), _F32) + b_ref[:, lanes]
            for s in range(SUBLANES):
                part = None
                for a in range((first + CONV_K + SUBLANES - 1) // SUBLANES):
                    k = SUBLANES * a + s - first
                    if 0 <= k < CONV_K:
                        term = hp_ref[pl.ds(t0 + SUBLANES * a, span), lanes] * w_ref[k:k + 1, lanes]
                        part = term if part is None else part + term
                acc = acc + part[s:s + _CONV_ROWS, :]
            halves.append(acc)
        acc = jnp.concatenate(halves, axis=1)
        mu = jnp.mean(acc, axis=-1, keepdims=True)
        cen = acc - mu
        var = jnp.mean(cen * cen, axis=-1, keepdims=True)
        y = cen * lax.rsqrt(var + EPS) * g_ref[...] + beta_ref[...]
        o_ref[pl.ds(t0, _CONV_ROWS), :] = (y * jax.nn.sigmoid(y)).astype(o_ref.dtype)
        return carry

    lax.fori_loop(0, seq // _CONV_ROWS, conv_body, 0, unroll=2)


def _conv(u, layer, dw_w, dw_b, ln_g, ln_b, *, seq):
    nb = u.shape[0] // seq
    u3 = u.reshape(nb, seq, CONV_CH)
    vec = pl.BlockSpec((None, 1, CONV_CH), lambda b: (layer, 0, 0))
    out = pl.pallas_call(
        functools.partial(_conv_kernel, seq=seq),
        grid=(nb,),
        in_specs=[
            pl.BlockSpec((None, seq, CONV_CH), lambda b: (b, 0, 0)),
            pl.BlockSpec((None, CONV_K, CONV_CH), lambda b: (layer, 0, 0)),
            vec, vec, vec,
        ],
        out_specs=pl.BlockSpec((None, seq, CONV_CH), lambda b: (b, 0, 0)),
        out_shape=jax.ShapeDtypeStruct((nb, seq, CONV_CH), _BF16),
        scratch_shapes=[pltpu.VMEM((seq + 2 * _CONV_PAD, CONV_CH), _F32)],
        compiler_params=_params(("arbitrary",), 32),
        name="conv_module",
    )(u3, dw_w, dw_b, ln_g, ln_b)
    return out.reshape(nb * seq, CONV_CH)


def _softmax_pv_t(s, value_t_blocks, key_splits):
    m = jnp.max(s, axis=-1, keepdims=True)
    e = jnp.exp2(s - m)
    denom = jnp.sum(e, axis=-1, keepdims=True)
    eb = e.astype(_BF16)
    out = None
    for i, vt in enumerate(value_t_blocks):
        pv = _dot_nt(eb[:, key_splits[i]:key_splits[i + 1]], vt)
        out = pv if out is None else out + pv
    return out / denom


_SOFTMAX_ROWS = 16
_SOFTMAX_LANES = 768


def _softmax_to(s_ref, p_ref):
    n, width = s_ref.shape
    sums = []
    for r0 in range(0, n, _SOFTMAX_ROWS):
        rows = slice(r0, r0 + _SOFTMAX_ROWS)
        m = jnp.max(s_ref[rows, :], axis=-1, keepdims=True)
        part = None
        for l0 in range(0, width, _SOFTMAX_LANES):
            l1 = min(l0 + _SOFTMAX_LANES, width)
            e = jnp.exp2(s_ref[rows, l0:l1] - m)
            p_ref[rows, l0:l1] = e.astype(p_ref.dtype)
            for j in range(0, l1 - l0, LANES):
                part = e[:, j:j + LANES] if part is None else part + e[:, j:j + LANES]
        sums.append(jnp.sum(part, axis=-1, keepdims=True))
    return jnp.concatenate(sums, axis=0)


def _staging_views(*refs):
    zero = jnp.minimum(pl.program_id(0), 0)
    return tuple(ref.at[zero] for ref in refs)


def _run_pipeline(n, scores, softmax, weighted_values, score_buffers):
    scores(0)
    softmax(0)
    for u in range(1, min(score_buffers, n)):
        scores(u)
    for u in range(n):
        if u + 1 < n:
            softmax(u + 1)
        weighted_values(u)
        if u + score_buffers < n:
            scores(u + score_buffers)


def _head_rows(block, parity):
    rows = lax.broadcasted_iota(jnp.int32, block.shape, 0)
    keep = rows < HEAD_DIM if parity == 0 else rows >= HEAD_DIM
    return jnp.where(keep, block, jnp.zeros_like(block))


def _pair_block(ref, idx, pair):
    return ref[idx + (slice(2 * pair, 2 * pair + 2),)].reshape(PAIR, KV_BLK)


def _prompt_attn_kernel(qa_ref, ka_ref, va_ref, qn_ref, kn_ref, vn_ref, b_ref, c_ref):
    low_out = _low_half((SEQ, LANES))
    zero = jnp.zeros((HEAD_DIM, SEQ), _BF16)
    for p in range(GQA_HEADS // 2):
        cols = slice(p * LANES, (p + 1) * LANES)
        q = qa_ref[:, cols]
        outs = []
        for parity in range(2):
            kv = (2 * p + parity) // GQA_GROUP
            k = ka_ref[kv].astype(_BF16)
            v = va_ref[kv].astype(_BF16)
            k_sel = jnp.concatenate([k, zero] if parity == 0 else [zero, k], axis=0)
            outs.append(_softmax_pv_t(_dot(q, k_sel), [jnp.concatenate([v, v], axis=0)], (0, SEQ)))
        b_ref[:, cols] = jnp.where(low_out, outs[0], outs[1]).astype(b_ref.dtype)
    for p in range(NA_HEADS // 2):
        cols = slice(p * LANES, (p + 1) * LANES)
        q = qn_ref[:, cols]
        k = _pair_block(kn_ref, (), p).astype(_BF16)
        v = _pair_block(vn_ref, (), p).astype(_BF16)
        outs = [_softmax_pv_t(_dot(q, _head_rows(k, parity)), [v], (0, SEQ)) for parity in range(2)]
        c_ref[:, cols] = jnp.where(low_out, outs[0], outs[1]).astype(c_ref.dtype)


def _prompt_attention(qa, ka_t, va_t, qn, kn_t, vn_t, layer):
    def tok(w):
        return pl.BlockSpec((SEQ, w), lambda b: (b, 0))

    def ctx(nh):
        return pl.BlockSpec((None, None, nh, HEAD_DIM, SEQ), lambda b: (b, layer, 0, 0, 0))

    return pl.pallas_call(
        _prompt_attn_kernel,
        grid=(BATCH,),
        in_specs=[tok(ATTN_W), ctx(GQA_KV_HEADS), ctx(GQA_KV_HEADS), tok(ATTN_W), ctx(NA_HEADS), ctx(NA_HEADS)],
        out_specs=[tok(ATTN_W), tok(ATTN_W)],
        out_shape=[jax.ShapeDtypeStruct((BATCH * SEQ, ATTN_W), _BF16)] * 2,
        compiler_params=_params(("arbitrary",), 32),
        name="prompt_attention",
    )(qa, ka_t, va_t, qn, kn_t, vn_t)


_GQA_TQ = 512
_GQA_UNIT = 256
_GQA_KEYS = DEC_SEQ + PAST_LEN


def _gqa_kernel(q_ref, k_ref, v_ref, kc_ref, vc_ref, o_ref, ksel_ref, vdup_ref,
                s0_ref, s1_ref, s2_ref, p0_ref, p1_ref):
    @pl.when(pl.program_id(1) == 0)
    def _():
        ksel_ref[...] = jnp.zeros(ksel_ref.shape, _BF16)
        for blk in range(KV_BLOCKS + 1):
            lanes = slice(blk * KV_BLK, (blk + 1) * KV_BLK)
            for kv in range(GQA_KV_HEADS):
                if blk < KV_BLOCKS:
                    k, v = k_ref[blk, kv], v_ref[blk, kv]
                else:
                    k, v = kc_ref[kv].astype(_BF16), vc_ref[kv].astype(_BF16)
                ksel_ref[2 * kv, 0:HEAD_DIM, lanes] = k
                ksel_ref[2 * kv + 1, HEAD_DIM:PAIR, lanes] = k
                vdup_ref[kv, 0:HEAD_DIM, lanes] = v
                vdup_ref[kv, HEAD_DIM:PAIR, lanes] = v

    low_out = _low_half((_GQA_UNIT, LANES))
    s_refs = _staging_views(s0_ref, s1_ref, s2_ref)
    p_refs = _staging_views(p0_ref, p1_ref)
    denoms, outs = {}, {}
    units = [(g, h) for g in range(_GQA_TQ // _GQA_UNIT) for h in range(GQA_HEADS)]

    def scores(u):
        g, head = units[u]
        p = head // 2
        kid = 2 * (head // GQA_GROUP) + head % 2
        q = q_ref[g * _GQA_UNIT:(g + 1) * _GQA_UNIT, p * LANES:(p + 1) * LANES]
        s_refs[u % len(s_refs)][...] = _dot(q, ksel_ref[kid])

    def softmax(u):
        denoms[u] = _softmax_to(s_refs[u % len(s_refs)], p_refs[u % 2])

    def weighted_values(u):
        g, head = units[u]
        outs[u] = _dot_nt(p_refs[u % 2][...], vdup_ref[head // GQA_GROUP]) / denoms[u]
        if head % 2 == 1:
            p = head // 2
            o_ref[g * _GQA_UNIT:(g + 1) * _GQA_UNIT, p * LANES:(p + 1) * LANES] = (
                jnp.where(low_out, outs[u - 1], outs[u]).astype(o_ref.dtype))

    _run_pipeline(len(units), scores, softmax, weighted_values, len(s_refs))


def _sample_gqa(q, k_t, v_t, cache_k_t, cache_v_t, layer):
    tiles = DEC_SEQ // _GQA_TQ
    new_spec = pl.BlockSpec((KV_BLOCKS, GQA_KV_HEADS, HEAD_DIM, KV_BLK), lambda b, i: (b, 0, 0, 0))
    cache_spec = pl.BlockSpec((None, None, GQA_KV_HEADS, HEAD_DIM, PAST_LEN), lambda b, i: (b, layer, 0, 0, 0))
    return pl.pallas_call(
        _gqa_kernel,
        grid=(DEC_BATCH, tiles),
        in_specs=[pl.BlockSpec((_GQA_TQ, ATTN_W), lambda b, i: (b * tiles + i, 0)),
                  new_spec, new_spec, cache_spec, cache_spec],
        out_specs=pl.BlockSpec((_GQA_TQ, ATTN_W), lambda b, i: (b * tiles + i, 0)),
        out_shape=jax.ShapeDtypeStruct((DEC_BATCH * DEC_SEQ, ATTN_W), _BF16),
        scratch_shapes=[pltpu.VMEM((4, PAIR, _GQA_KEYS), _BF16), pltpu.VMEM((2, PAIR, _GQA_KEYS), _BF16),
                        pltpu.VMEM((1, _GQA_UNIT, _GQA_KEYS), _F32), pltpu.VMEM((1, _GQA_UNIT, _GQA_KEYS), _F32),
                        pltpu.VMEM((1, _GQA_UNIT, _GQA_KEYS), _F32),
                        pltpu.VMEM((1, _GQA_UNIT, _GQA_KEYS), _BF16), pltpu.VMEM((1, _GQA_UNIT, _GQA_KEYS), _BF16)],
        compiler_params=_params(("arbitrary", "arbitrary"), 48),
        name="sample_gqa",
    )(q, k_t, v_t, cache_k_t, cache_v_t)


_NA_GROUP = 4
_NA_GROUPS = GRID_H // _NA_GROUP
_NA_GROUPS_PER_STEP = 4
_NA_WIN_ROWS = 12
_NA_Q = _NA_GROUP * GRID_W
_NA_KEYS = _NA_WIN_ROWS * GRID_W
_NA_WIN_BLOCKS = _NA_KEYS // KV_BLK
_NA_VARIANTS = 3
_NA_OFFSETS = 2 * NA_WIN_R - 1
_NA_RPB_ROWS = 16

assert _NA_GROUP * GRID_W == KV_BLK


def _na_window_start(group):
    lo, hi = 0, GRID_H - _NA_WIN_ROWS
    start = _NA_GROUP * group - NA_WIN_R // 2
    if isinstance(group, int):
        return min(max(start, lo), hi)
    return jnp.clip(start, lo, hi)


def _na_row_plan():
    plan = []
    for group in (0, 1, _NA_GROUPS - 1):
        w0 = _na_window_start(group)
        rows = []
        for i in range(_NA_GROUP):
            r = _NA_GROUP * group + i
            rs = min(max(r - NA_WIN_R // 2, 0), GRID_H - NA_WIN_R)
            rows.append([kr - r + NA_WIN_R - 1 if rs <= kr < rs + NA_WIN_R else None
                         for kr in range(w0, w0 + _NA_WIN_ROWS)])
        plan.append(rows)
    return plan


def _na_build_bias(rpb_ref, bias_ref, t2_ref):
    shape = (GRID_W, LANES)
    c = lax.broadcasted_iota(jnp.int32, shape, 0)
    lane = lax.broadcasted_iota(jnp.int32, shape, 1)
    kc = lane % GRID_W
    cs = jnp.clip(c - NA_WIN_C // 2, 0, GRID_W - NA_WIN_C)
    col_ok = (kc >= cs) & (kc < cs + NA_WIN_C)
    low = lane < GRID_W
    neg = jnp.full(shape, NEG_INF, _F32)
    plan = _na_row_plan()

    def head_body(h, carry):
        for d in range(_NA_OFFSETS):
            v = jnp.broadcast_to(rpb_ref[h, d:d + 1, :], shape)
            lo = pltpu.roll(v, LANES - (NA_WIN_C - 1), 1, stride=1, stride_axis=0)
            hi = pltpu.roll(v, GRID_W - (NA_WIN_C - 1), 1, stride=1, stride_axis=0)
            t2_ref[d] = jnp.where(col_ok, jnp.where(low, lo, hi) * LOG2E, neg)
        for variant in range(_NA_VARIANTS):
            for i in range(_NA_GROUP):
                for m in range(_NA_WIN_ROWS // 2):
                    d_even, d_odd = plan[variant][i][2 * m], plan[variant][i][2 * m + 1]
                    even = neg if d_even is None else t2_ref[d_even]
                    odd = neg if d_odd is None else t2_ref[d_odd]
                    tile = neg if d_even is None and d_odd is None else jnp.where(low, even, odd)
                    bias_ref[variant, h, i * GRID_W:(i + 1) * GRID_W, m * LANES:(m + 1) * LANES] = tile
        return carry

    lax.fori_loop(0, NA_HEADS, head_body, 0)


def _na_kernel(q_ref, k_ref, v_ref, kc_ref, vc_ref, rpb_ref, o_ref, bias_ref, t2_ref,
               s0_ref, s1_ref, s2_ref, p0_ref, p1_ref):
    step = pl.program_id(1)

    @pl.when((pl.program_id(0) == 0) & (step == 0))
    def _():
        _na_build_bias(rpb_ref, bias_ref, t2_ref)

    low_out = _low_half((_NA_Q, LANES))
    s_refs = _staging_views(s0_ref, s1_ref, s2_ref)
    p_refs = _staging_views(p0_ref, p1_ref)
    denoms, outs = {}, {}
    units = [(g, h) for g in range(_NA_GROUPS_PER_STEP) for h in range(NA_HEADS)]

    def geometry(g):
        group = step * _NA_GROUPS_PER_STEP + g
        variant = jnp.where(group == 0, 0, jnp.where(group == _NA_GROUPS - 1, 2, 1))
        return variant, _na_window_start(group) // _NA_GROUP

    geo = [geometry(g) for g in range(_NA_GROUPS_PER_STEP)]

    def scores(u):
        g, h = units[u]
        variant, blk0 = geo[g]
        p, parity = divmod(h, 2)
        q = q_ref[g * _NA_Q:(g + 1) * _NA_Q, p * LANES:(p + 1) * LANES]
        s_ref = s_refs[u % len(s_refs)]
        for w in range(_NA_WIN_BLOCKS):
            lanes = slice(w * KV_BLK, (w + 1) * KV_BLK)
            k = _head_rows(_pair_block(k_ref, (blk0 + w,), p), parity)
            s_ref[:, lanes] = _dot(q, k) + bias_ref[variant, h, :, lanes]
        kc = _head_rows(_pair_block(kc_ref, (), p).astype(_BF16), parity)
        s_ref[:, _NA_KEYS:_NA_KEYS + PAST_LEN] = _dot(q, kc)

    def softmax(u):
        denoms[u] = _softmax_to(s_refs[u % len(s_refs)], p_refs[u % 2])

    def weighted_values(u):
        g, h = units[u]
        _, blk0 = geo[g]
        p = h // 2
        p_ref = p_refs[u % 2]
        pv = _dot_nt(p_ref[:, _NA_KEYS:_NA_KEYS + PAST_LEN], _pair_block(vc_ref, (), p).astype(_BF16))
        for w in range(_NA_WIN_BLOCKS):
            pv = pv + _dot_nt(p_ref[:, w * KV_BLK:(w + 1) * KV_BLK], _pair_block(v_ref, (blk0 + w,), p))
        outs[u] = pv / denoms[u]
        if h % 2 == 1:
            o_ref[g * _NA_Q:(g + 1) * _NA_Q, p * LANES:(p + 1) * LANES] = (
                jnp.where(low_out, outs[u - 1], outs[u]).astype(o_ref.dtype))

    _run_pipeline(len(units), scores, softmax, weighted_values, len(s_refs))


def _sample_na(q, k_t, v_t, cache_k_t, cache_v_t, rpb, layer):
    steps = _NA_GROUPS // _NA_GROUPS_PER_STEP
    tq = _NA_GROUPS_PER_STEP * _NA_Q
    new_spec = pl.BlockSpec((KV_BLOCKS, NA_HEADS, HEAD_DIM, KV_BLK), lambda b, i: (b, 0, 0, 0))
    cache_spec = pl.BlockSpec((None, None, NA_HEADS, HEAD_DIM, PAST_LEN), lambda b, i: (b, layer, 0, 0, 0))
    return pl.pallas_call(
        _na_kernel,
        grid=(DEC_BATCH, steps),
        in_specs=[pl.BlockSpec((tq, ATTN_W), lambda b, i: (b * steps + i, 0)),
                  new_spec, new_spec, cache_spec, cache_spec,
                  pl.BlockSpec((None, NA_HEADS, _NA_RPB_ROWS, LANES), lambda b, i: (layer, 0, 0, 0))],
        out_specs=pl.BlockSpec((tq, ATTN_W), lambda b, i: (b * steps + i, 0)),
        out_shape=jax.ShapeDtypeStruct((DEC_BATCH * DEC_SEQ, ATTN_W), _BF16),
        scratch_shapes=[pltpu.VMEM((_NA_VARIANTS, NA_HEADS, _NA_Q, _NA_KEYS), _F32),
                        pltpu.VMEM((_NA_OFFSETS, GRID_W, LANES), _F32),
                        pltpu.VMEM((1, _NA_Q, _NA_KEYS + PAST_LEN), _F32),
                        pltpu.VMEM((1, _NA_Q, _NA_KEYS + PAST_LEN), _F32),
                        pltpu.VMEM((1, _NA_Q, _NA_KEYS + PAST_LEN), _F32),
                        pltpu.VMEM((1, _NA_Q, _NA_KEYS + PAST_LEN), _BF16),
                        pltpu.VMEM((1, _NA_Q, _NA_KEYS + PAST_LEN), _BF16)],
        compiler_params=_params(("arbitrary", "arbitrary"), 48),
        name="sample_na",
    )(q, k_t, v_t, cache_k_t, cache_v_t, rpb)


def _out_mlp_kernel(a_ref, b_ref, c_ref, x_ref, g1_ref, sh2_ref, sc2_ref, g2_ref, n2_ref,
                    wo_ref, w1_ref, w2_ref, o_ref):
    b0 = CONV_CH
    c0 = CONV_CH + ATTN_W
    y = (_dot(a_ref[...], wo_ref[0:b0, :]) + _dot(b_ref[...], wo_ref[b0:c0, :])
         + _dot(c_ref[...], wo_ref[c0:D_MODEL, :]))
    x1 = x_ref[...] + g1_ref[...] * y
    ms = jnp.mean(x1 * x1, axis=-1, keepdims=True)
    h = (x1 * lax.rsqrt(ms + EPS)) * n2_ref[...]
    hb = (h * (1.0 + sc2_ref[...]) + sh2_ref[...]).astype(_BF16)
    acc = jnp.zeros(x1.shape, _F32)
    for j in range(D_FF // FF_CHUNK):
        t = jnp.maximum(_dot(hb, w1_ref[:, j * FF_CHUNK:(j + 1) * FF_CHUNK]), 0.0)
        acc = acc + _dot((t * t).astype(_BF16), w2_ref[j * FF_CHUNK:(j + 1) * FF_CHUNK, :])
    o_ref[...] = x1 + g2_ref[...] * acc


def _out_mlp(a, b, c, x, mod4, layer, norm2_g, w_out, w1, w2, *, sample):
    tokens = x.shape[0]
    tm = TM
    row = _row_fn(sample, tm)

    def tok(w):
        return pl.BlockSpec((tm, w), lambda i: (i, 0))

    return pl.pallas_call(
        _out_mlp_kernel,
        grid=(tokens // tm,),
        in_specs=[tok(CONV_CH), tok(ATTN_W), tok(ATTN_W), tok(D_MODEL),
                  _mod_spec(layer, row, 2), _mod_spec(layer, row, 3),
                  _mod_spec(layer, row, 4), _mod_spec(layer, row, 5),
                  pl.BlockSpec((None, 1, D_MODEL), lambda i: (layer, 0, 0)),
                  pl.BlockSpec((None, D_MODEL, D_MODEL), lambda i: (layer, 0, 0), pipeline_mode=pl.Buffered(1)),
                  pl.BlockSpec((None, D_MODEL, D_FF), lambda i: (layer, 0, 0), pipeline_mode=pl.Buffered(1)),
                  pl.BlockSpec((None, D_FF, D_MODEL), lambda i: (layer, 0, 0), pipeline_mode=pl.Buffered(1))],
        out_specs=tok(D_MODEL),
        out_shape=jax.ShapeDtypeStruct((tokens, D_MODEL), _F32),
        compiler_params=_params(("arbitrary",), 56),
        name="out_mlp",
    )(a, b, c, x, mod4, mod4, mod4, mod4, norm2_g, w_out, w1, w2)


def kernel(x_prompt, x_sample, cache_attn_k, cache_attn_v, cache_na_k, cache_na_v, c, c_ctx, ada_w, ada_b, norm1_g, norm2_g, w_in, conv_dw_w, conv_dw_b, conv_ln_g, conv_ln_b, attn_q_g, attn_k_g, na_q_g, na_k_g, na_rpb, w_out, mlp_w1, mlp_w2):
    cvec = jnp.concatenate(
        [c_ctx[None, :], c, jnp.zeros((MOD_ROWS - 1 - DEC_BATCH, D_MODEL), _F32)], axis=0)
    mod4 = _modulation(cvec, ada_w, ada_b).reshape(DEPTH, MOD_ROWS, 1, 6 * D_MODEL)

    def heads(g, n):
        return jnp.tile(g, (1, n))
    gains = jnp.concatenate(
        [heads(attn_q_g, GQA_HEADS) * Q_SCALE, heads(attn_k_g, GQA_KV_HEADS),
         heads(na_q_g, NA_HEADS) * Q_SCALE, heads(na_k_g, NA_HEADS)], axis=1)[:, None, :]
    rope_tabs = _rope_tables()
    rpb = jnp.pad(na_rpb, ((0, 0), (0, 0), (0, _NA_RPB_ROWS - _NA_OFFSETS),
                           (0, LANES - (2 * NA_WIN_C - 1))))

    w_in_b = w_in.astype(_BF16)
    w_out_b = w_out.astype(_BF16)
    w1_b = mlp_w1.astype(_BF16)
    w2_b = mlp_w2.astype(_BF16)
    n1 = norm1_g.reshape(DEPTH, 1, D_MODEL)
    n2 = norm2_g.reshape(DEPTH, 1, D_MODEL)
    dw_b = conv_dw_b.reshape(DEPTH, 1, CONV_CH)
    ln_g = conv_ln_g.reshape(DEPTH, 1, CONV_CH)
    ln_b = conv_ln_b.reshape(DEPTH, 1, CONV_CH)
    to_t = (0, 1, 3, 4, 2)
    ck_a, cv_a, ck_n, cv_n = (t.transpose(to_t) for t in (cache_attn_k, cache_attn_v, cache_na_k, cache_na_v))

    xp = x_prompt.reshape(BATCH * SEQ, D_MODEL)
    xs = x_sample.reshape(DEC_BATCH * DEC_SEQ, D_MODEL)
    ctx = None
    for layer in range(DEPTH):
        u, qa, ka, va, qn, kn, vn = _in_proj(xp, mod4, layer, n1, w_in_b, gains, None, ctx, sample=False)
        ctx = (ka, va, kn, vn)
        a_out = _conv(u, layer, conv_dw_w, dw_b, ln_g, ln_b, seq=SEQ)
        b_out, c_out = _prompt_attention(qa, ka, va, qn, kn, vn, layer)
        xp = _out_mlp(a_out, b_out, c_out, xp, mod4, layer, n2, w_out_b, w1_b, w2_b, sample=False)
        u, qa, ka, va, qn, kn, vn = _in_proj(xs, mod4, layer, n1, w_in_b, gains, rope_tabs, None, sample=True)
        a_out = _conv(u, layer, conv_dw_w, dw_b, ln_g, ln_b, seq=DEC_SEQ)
        b_out = _sample_gqa(qa, ka, va, ck_a, cv_a, layer)
        c_out = _sample_na(qn, kn, vn, ck_n, cv_n, rpb, layer)
        xs = _out_mlp(a_out, b_out, c_out, xs, mod4, layer, n2, w_out_b, w1_b, w2_b, sample=True)

    outs = [t.transpose(0, 1, 4, 2, 3) for t in ctx]
    return (xp.reshape(BATCH, SEQ, D_MODEL), xs.reshape(DEC_BATCH, DEC_SEQ, D_MODEL), *outs)
```

```python
import functools
import math

import numpy as np
import jax
import jax.numpy as jnp
from jax import lax
from jax.experimental import pallas as pl
from jax.experimental.pallas import tpu as pltpu

D_MODEL = 1024
BATCH = 32
SEQ = 256
DEPTH = 4
DEC_BATCH = 8
DEC_SEQ = 2048
PAST_LEN = 256
GRID_W = 64
GRID_H = DEC_SEQ // GRID_W
HEAD_DIM = 64
CONV_CH = D_MODEL // 4
CONV_K = 31
ATTN_W = (D_MODEL - CONV_CH) // 2
GQA_HEADS = ATTN_W // HEAD_DIM
GQA_KV_HEADS = GQA_HEADS // 3
GQA_GROUP = GQA_HEADS // GQA_KV_HEADS
NA_HEADS = ATTN_W // HEAD_DIM
D_FF = 4 * D_MODEL
NA_WIN_R = 8
NA_WIN_C = 16
ROPE_THETA = 10000.0
EPS = 1e-6
NEG_INF = -1e30

KV_W = GQA_KV_HEADS * HEAD_DIM
IN_W = 2 * CONV_CH + ATTN_W + 2 * KV_W + 3 * ATTN_W
_U0, _QA0, _KA0, _VA0, _QN0, _KN0, _VN0 = 0, 512, 896, 1024, 1152, 1536, 1920
LANES = 128
SUBLANES = 8
MXU_W = 256
MOD_ROWS = 16
LOG2E = math.log2(math.e)
Q_SCALE = HEAD_DIM ** -0.5 * LOG2E
_SUB_TILE = 512
TM = 2 * _SUB_TILE
FF_CHUNK = 512
KV_BLK = 256
KV_BLOCKS = DEC_SEQ // KV_BLK
PAIR = 2 * HEAD_DIM
MIB = 1024 * 1024

assert SEQ == KV_BLK and PAST_LEN == KV_BLK and _SUB_TILE % KV_BLK == 0 and PAIR == LANES
assert DEC_SEQ % TM == 0 and (BATCH * SEQ) % TM == 0

_F32 = jnp.float32
_BF16 = jnp.bfloat16


def _params(semantics, vmem_mib):
    return pltpu.CompilerParams(dimension_semantics=semantics, vmem_limit_bytes=vmem_mib * MIB)


def _dot(a, b):
    return jnp.dot(a, b, preferred_element_type=_F32)


def _dot_nt(a, b):
    return lax.dot_general(a, b, (((1,), (1,)), ((), ())), preferred_element_type=_F32)


def _low_half(shape):
    return lax.broadcasted_iota(jnp.int32, shape, len(shape) - 1) % LANES < HEAD_DIM


def _mod_kernel(c_ref, w_ref, b_ref, o_ref):
    c = c_ref[...]
    s = (c * jax.nn.sigmoid(c)).astype(_BF16)
    o_ref[...] = _dot(s, w_ref[...].astype(_BF16)) + b_ref[...]


def _modulation(cvec, ada_w, ada_b):
    tn = 1536
    return pl.pallas_call(
        _mod_kernel,
        grid=(DEPTH, 6 * D_MODEL // tn),
        in_specs=[
            pl.BlockSpec((MOD_ROWS, D_MODEL), lambda l, j: (0, 0)),
            pl.BlockSpec((None, D_MODEL, tn), lambda l, j: (l, 0, j)),
            pl.BlockSpec((None, 1, tn), lambda l, j: (l, 0, j)),
        ],
        out_specs=pl.BlockSpec((None, MOD_ROWS, tn), lambda l, j: (l, 0, j)),
        out_shape=jax.ShapeDtypeStruct((DEPTH, MOD_ROWS, 6 * D_MODEL), _F32),
        compiler_params=_params(("arbitrary", "arbitrary"), 40),
        name="modulation",
    )(cvec, ada_w, ada_b.reshape(DEPTH, 1, 6 * D_MODEL))


def _mod_spec(layer, row_fn, part):
    return pl.BlockSpec((None, None, 1, D_MODEL), lambda i: (layer, row_fn(i), 0, part))


def _row_fn(sample, tm):
    tiles_per_seq = DEC_SEQ // tm
    if sample:
        return lambda i: 1 + i // tiles_per_seq
    return lambda i: 0


def _rope_tables():
    t = np.arange(DEC_SEQ)
    pos = (t // GRID_W, t % GRID_W)
    half = HEAD_DIM // 2
    quarter = half // 2
    inv = 1.0 / (ROPE_THETA ** (np.arange(quarter) * 2.0 / half))
    lane = np.arange(LANES)
    d = lane % HEAD_DIM
    axis = d // half
    second = (d % half) >= quarter
    p = np.where(axis[None, :] == 0, pos[0][:, None], pos[1][:, None]).astype(np.float64)
    ang = p * inv[d % quarter][None, :]
    cos, sin = np.cos(ang), np.sin(ang)
    sin_next = np.where(second[None, :], 0.0, -sin)
    sin_prev = np.where(second[None, :], sin, 0.0)
    return tuple(jnp.asarray(a, _F32) for a in (cos, sin_next, sin_prev))


_GAIN_W = 2 * ATTN_W + KV_W + ATTN_W


def _in_proj_kernel(*refs, rope, n_alias):
    n_in = 9 if rope else 6
    if rope:
        x_ref, sh_ref, sc_ref, n1_ref, w_ref, g_ref, cos_ref, sn_ref, sp_ref = refs[:n_in]
    else:
        x_ref, sh_ref, sc_ref, n1_ref, w_ref, g_ref = refs[:n_in]
    u_ref, qa_ref, ka_ref, va_ref, qn_ref, kn_ref, vn_ref = refs[n_in + n_alias:]

    r = lax.broadcasted_iota(jnp.int32, (MXU_W, MXU_W), 0) // HEAD_DIM
    c = lax.broadcasted_iota(jnp.int32, (MXU_W, MXU_W), 1) // HEAD_DIM
    seg_mean = jnp.where(r == c, 1.0 / HEAD_DIM, 0.0).astype(_BF16)

    def normed_input(rows):
        x = x_ref[rows, :]
        ms = jnp.mean(x * x, axis=-1, keepdims=True)
        h = (x * lax.rsqrt(ms + EPS)) * n1_ref[...]
        return (h * (1.0 + sc_ref[...]) + sh_ref[...]).astype(_BF16)

    def project(sub, hb):
        rows = slice(sub * _SUB_TILE, (sub + 1) * _SUB_TILE)
        blk0 = sub * (_SUB_TILE // KV_BLK)

        y_all = _dot(hb, w_ref[...])

        def proj(c0, width):
            return y_all[:, c0:c0 + width]

        def rotate(blk):
            return (blk * cos_ref[rows, :] + pltpu.roll(blk, LANES - 16, 1) * sn_ref[rows, :]
                    + pltpu.roll(blk, 16, 1) * sp_ref[rows, :])

        def head_norm(c0, width, gain0, rotary):
            for b in range(width // MXU_W):
                y = proj(c0 + b * MXU_W, MXU_W)
                ss = _dot((y * y).astype(_BF16), seg_mean)
                g0 = gain0 + b * MXU_W
                y = y * lax.rsqrt(ss + EPS) * g_ref[:, g0:g0 + MXU_W]
                for j in range(MXU_W // LANES):
                    blk = y[:, j * LANES:(j + 1) * LANES]
                    yield (b * MXU_W) // LANES + j, rotate(blk) if rotary else blk

        def store_transposed(ref, pair, blk):
            t = blk.T
            for wblk in range(_SUB_TILE // KV_BLK):
                piece = t[:, wblk * KV_BLK:(wblk + 1) * KV_BLK].reshape(2, HEAD_DIM, KV_BLK)
                ref[blk0 + wblk, 2 * pair:2 * pair + 2] = piece.astype(ref.dtype)

        u = proj(_U0, 2 * CONV_CH)
        u_ref[rows, :] = (u[:, 0:CONV_CH] * jax.nn.sigmoid(u[:, CONV_CH:2 * CONV_CH])).astype(u_ref.dtype)
        q_blocks = ATTN_W // LANES
        for j, blk in head_norm(_QA0, ATTN_W + KV_W, 0, rope):
            if j < q_blocks:
                qa_ref[rows, j * LANES:(j + 1) * LANES] = blk.astype(qa_ref.dtype)
            else:
                store_transposed(ka_ref, 0, blk)
        store_transposed(va_ref, 0, proj(_VA0, KV_W))
        for j, blk in head_norm(_QN0, 2 * ATTN_W, ATTN_W + KV_W, False):
            if j < q_blocks:
                qn_ref[rows, j * LANES:(j + 1) * LANES] = blk.astype(qn_ref.dtype)
            else:
                store_transposed(kn_ref, j - q_blocks, blk)
        vn = proj(_VN0, ATTN_W)
        for j in range(ATTN_W // LANES):
            store_transposed(vn_ref, j, vn[:, j * LANES:(j + 1) * LANES])

    n_sub = x_ref.shape[0] // _SUB_TILE
    inputs = [normed_input(slice(s * _SUB_TILE, (s + 1) * _SUB_TILE)) for s in range(n_sub)]
    for s in range(n_sub):
        project(s, inputs[s])


def _in_proj(x, mod4, layer, norm1_g, w_in, gains, rope_tabs, prev_ctx, *, sample):
    tokens = x.shape[0]
    tm = TM
    blocks = tm // KV_BLK
    row = _row_fn(sample, tm)
    in_specs = [
        pl.BlockSpec((tm, D_MODEL), lambda i: (i, 0)),
        _mod_spec(layer, row, 0),
        _mod_spec(layer, row, 1),
        pl.BlockSpec((None, 1, D_MODEL), lambda i: (layer, 0, 0)),
        pl.BlockSpec((None, D_MODEL, IN_W), lambda i: (layer, 0, 0), pipeline_mode=pl.Buffered(1)),
        pl.BlockSpec((None, 1, _GAIN_W), lambda i: (layer, 0, 0)),
    ]
    args = [x, mod4, mod4, norm1_g, w_in, gains]
    aliases = {}
    kv_heads = (GQA_KV_HEADS, GQA_KV_HEADS, NA_HEADS, NA_HEADS)
    if sample:
        tiles_per_seq = DEC_SEQ // tm
        in_specs += [pl.BlockSpec((tm, LANES), lambda i: (i % tiles_per_seq, 0))] * 3
        args += list(rope_tabs)
        kv_specs = [pl.BlockSpec((blocks, nh, HEAD_DIM, KV_BLK), lambda i: (i, 0, 0, 0)) for nh in kv_heads]
        kv_shapes = [jax.ShapeDtypeStruct((tokens // KV_BLK, nh, HEAD_DIM, KV_BLK), _BF16) for nh in kv_heads]
    else:
        kv_specs = [pl.BlockSpec((blocks, None, nh, HEAD_DIM, KV_BLK), lambda i: (i, layer, 0, 0, 0))
                    for nh in kv_heads]
        kv_shapes = [jax.ShapeDtypeStruct((BATCH, DEPTH, nh, HEAD_DIM, SEQ), _F32) for nh in kv_heads]
        kv_out_index = (2, 3, 5, 6)
        for arr, out_idx in zip(prev_ctx, kv_out_index):
            aliases[len(args)] = out_idx
            in_specs.append(pl.BlockSpec(memory_space=pl.ANY))
            args.append(arr)

    def tok(w):
        return pl.BlockSpec((tm, w), lambda i: (i, 0))

    def tok_shape(w):
        return jax.ShapeDtypeStruct((tokens, w), _BF16)

    out_specs = [tok(CONV_CH), tok(ATTN_W), kv_specs[0], kv_specs[1], tok(ATTN_W), kv_specs[2], kv_specs[3]]
    out_shape = [tok_shape(CONV_CH), tok_shape(ATTN_W), kv_shapes[0], kv_shapes[1],
                 tok_shape(ATTN_W), kv_shapes[2], kv_shapes[3]]
    return pl.pallas_call(
        functools.partial(_in_proj_kernel, rope=sample, n_alias=len(aliases)),
        grid=(tokens // tm,),
        in_specs=in_specs,
        out_specs=out_specs,
        out_shape=out_shape,
        input_output_aliases=aliases,
        compiler_params=_params(("arbitrary",), 48),
        name="in_proj_sample" if sample else "in_proj_prompt",
    )(*args)


_CONV_ROWS = 64
_CONV_PAD = 16


def _conv_chunk(hp_ref, t0, w_ref, b_ref, g_ref, beta_ref):
    first = _CONV_PAD - CONV_K // 2
    span = _CONV_ROWS + SUBLANES
    halves = []
    for c0 in range(0, CONV_CH, LANES):
        lanes = slice(c0, c0 + LANES)
        acc = jnp.zeros((_CONV_ROWS, LANES), _F32) + b_ref[:, lanes]
        for s in range(SUBLANES):
            part = None
            for a in range((first + CONV_K + SUBLANES - 1) // SUBLANES):
                k = SUBLANES * a + s - first
                if 0 <= k < CONV_K:
                    r0 = t0 + SUBLANES * a
                    term = hp_ref[r0:r0 + span, lanes] * w_ref[k:k + 1, lanes]
                    part = term if part is None else part + term
            acc = acc + part[s:s + _CONV_ROWS, :]
        halves.append(acc)
    acc = jnp.concatenate(halves, axis=1)
    mu = jnp.mean(acc, axis=-1, keepdims=True)
    cen = acc - mu
    var = jnp.mean(cen * cen, axis=-1, keepdims=True)
    y = cen * lax.rsqrt(var + EPS) * g_ref[...] + beta_ref[...]
    return y * jax.nn.sigmoid(y)


_SOFTMAX_ROWS = 16
_SOFTMAX_LANES = 768


def _softmax_to(s_ref, p_ref):
    n, width = s_ref.shape
    sums = []
    for r0 in range(0, n, _SOFTMAX_ROWS):
        rows = slice(r0, r0 + _SOFTMAX_ROWS)
        m = jnp.max(s_ref[rows, :], axis=-1, keepdims=True)
        part = None
        for l0 in range(0, width, _SOFTMAX_LANES):
            l1 = min(l0 + _SOFTMAX_LANES, width)
            e = jnp.exp2(s_ref[rows, l0:l1] - m)
            p_ref[rows, l0:l1] = e.astype(p_ref.dtype)
            for j in range(0, l1 - l0, LANES):
                part = e[:, j:j + LANES] if part is None else part + e[:, j:j + LANES]
        sums.append(jnp.sum(part, axis=-1, keepdims=True))
    return jnp.concatenate(sums, axis=0)


def _staging_views(*refs):
    zero = jnp.minimum(pl.program_id(0), 0)
    return tuple(ref.at[zero] for ref in refs)


def _run_pipeline(n, scores, softmax, weighted_values, score_buffers):
    scores(0)
    softmax(0)
    for u in range(1, min(score_buffers, n)):
        scores(u)
    for u in range(n):
        if u + 1 < n:
            softmax(u + 1)
        weighted_values(u)
        if u + score_buffers < n:
            scores(u + score_buffers)


def _head_rows(block, parity):
    rows = lax.broadcasted_iota(jnp.int32, block.shape, 0)
    keep = rows < HEAD_DIM if parity == 0 else rows >= HEAD_DIM
    return jnp.where(keep, block, jnp.zeros_like(block))


def _pair_block(ref, idx, pair):
    return ref[idx + (slice(2 * pair, 2 * pair + 2),)].reshape(PAIR, KV_BLK)


def _softmax_pv_t(s, value_t):
    m = jnp.max(s, axis=-1, keepdims=True)
    e = jnp.exp2(s - m)
    return _dot_nt(e.astype(_BF16), value_t) / jnp.sum(e, axis=-1, keepdims=True)


def _prompt_attn_kernel(qa_ref, ka_ref, va_ref, qn_ref, kn_ref, vn_ref, b_ref, c_ref):
    low_out = _low_half((SEQ, LANES))
    zero = jnp.zeros((HEAD_DIM, SEQ), _BF16)
    for p in range(GQA_HEADS // 2):
        cols = slice(p * LANES, (p + 1) * LANES)
        q = qa_ref[:, cols]
        outs = []
        for parity in range(2):
            kv = (2 * p + parity) // GQA_GROUP
            k = ka_ref[kv].astype(_BF16)
            v = va_ref[kv].astype(_BF16)
            k_sel = jnp.concatenate([k, zero] if parity == 0 else [zero, k], axis=0)
            outs.append(_softmax_pv_t(_dot(q, k_sel), jnp.concatenate([v, v], axis=0)))
        b_ref[:, cols] = jnp.where(low_out, outs[0], outs[1]).astype(b_ref.dtype)
    for p in range(NA_HEADS // 2):
        cols = slice(p * LANES, (p + 1) * LANES)
        q = qn_ref[:, cols]
        k = _pair_block(kn_ref, (), p).astype(_BF16)
        v = _pair_block(vn_ref, (), p).astype(_BF16)
        outs = [_softmax_pv_t(_dot(q, _head_rows(k, parity)), v) for parity in range(2)]
        c_ref[:, cols] = jnp.where(low_out, outs[0], outs[1]).astype(c_ref.dtype)


def _prompt_attention(qa, ka_t, va_t, qn, kn_t, vn_t, layer):
    def tok(w):
        return pl.BlockSpec((SEQ, w), lambda b: (b, 0))

    def ctx(nh):
        return pl.BlockSpec((None, None, nh, HEAD_DIM, SEQ), lambda b: (b, layer, 0, 0, 0))

    return pl.pallas_call(
        _prompt_attn_kernel,
        grid=(BATCH,),
        in_specs=[tok(ATTN_W), ctx(GQA_KV_HEADS), ctx(GQA_KV_HEADS), tok(ATTN_W), ctx(NA_HEADS), ctx(NA_HEADS)],
        out_specs=[tok(ATTN_W), tok(ATTN_W)],
        out_shape=[jax.ShapeDtypeStruct((BATCH * SEQ, ATTN_W), _BF16)] * 2,
        compiler_params=_params(("arbitrary",), 32),
        name="prompt_attention",
    )(qa, ka_t, va_t, qn, kn_t, vn_t)


_GQA_TQ = 512
_GQA_UNIT = 256
_GQA_KEYS = DEC_SEQ + PAST_LEN


def _gqa_kernel(q_ref, k_ref, v_ref, kc_ref, vc_ref, o_ref, ksel_ref, vdup_ref,
                s0_ref, s1_ref, s2_ref, p0_ref, p1_ref):
    @pl.when(pl.program_id(1) == 0)
    def _():
        ksel_ref[...] = jnp.zeros(ksel_ref.shape, _BF16)
        for blk in range(KV_BLOCKS + 1):
            lanes = slice(blk * KV_BLK, (blk + 1) * KV_BLK)
            for kv in range(GQA_KV_HEADS):
                if blk < KV_BLOCKS:
                    k, v = k_ref[blk, kv], v_ref[blk, kv]
                else:
                    k, v = kc_ref[kv].astype(_BF16), vc_ref[kv].astype(_BF16)
                ksel_ref[2 * kv, 0:HEAD_DIM, lanes] = k
                ksel_ref[2 * kv + 1, HEAD_DIM:PAIR, lanes] = k
                vdup_ref[kv, 0:HEAD_DIM, lanes] = v
                vdup_ref[kv, HEAD_DIM:PAIR, lanes] = v

    low_out = _low_half((_GQA_UNIT, LANES))
    s_refs = _staging_views(s0_ref, s1_ref, s2_ref)
    p_refs = _staging_views(p0_ref, p1_ref)
    denoms, outs = {}, {}
    units = [(g, h) for g in range(_GQA_TQ // _GQA_UNIT) for h in range(GQA_HEADS)]

    def scores(u):
        g, head = units[u]
        p = head // 2
        kid = 2 * (head // GQA_GROUP) + head % 2
        q = q_ref[g * _GQA_UNIT:(g + 1) * _GQA_UNIT, p * LANES:(p + 1) * LANES]
        s_refs[u % len(s_refs)][...] = _dot(q, ksel_ref[kid])

    def softmax(u):
        denoms[u] = _softmax_to(s_refs[u % len(s_refs)], p_refs[u % 2])

    def weighted_values(u):
        g, head = units[u]
        outs[u] = _dot_nt(p_refs[u % 2][...], vdup_ref[head // GQA_GROUP]) / denoms[u]
        if head % 2 == 1:
            p = head // 2
            o_ref[g * _GQA_UNIT:(g + 1) * _GQA_UNIT, p * LANES:(p + 1) * LANES] = (
                jnp.where(low_out, outs[u - 1], outs[u]).astype(o_ref.dtype))

    _run_pipeline(len(units), scores, softmax, weighted_values, len(s_refs))


def _sample_gqa(q, k_t, v_t, cache_k_t, cache_v_t, layer):
    tiles = DEC_SEQ // _GQA_TQ
    new_spec = pl.BlockSpec((KV_BLOCKS, GQA_KV_HEADS, HEAD_DIM, KV_BLK), lambda b, i: (b, 0, 0, 0))
    cache_spec = pl.BlockSpec((None, None, GQA_KV_HEADS, HEAD_DIM, PAST_LEN), lambda b, i: (b, layer, 0, 0, 0))
    return pl.pallas_call(
        _gqa_kernel,
        grid=(DEC_BATCH, tiles),
        in_specs=[pl.BlockSpec((_GQA_TQ, ATTN_W), lambda b, i: (b * tiles + i, 0)),
                  new_spec, new_spec, cache_spec, cache_spec],
        out_specs=pl.BlockSpec((_GQA_TQ, ATTN_W), lambda b, i: (b * tiles + i, 0)),
        out_shape=jax.ShapeDtypeStruct((DEC_BATCH * DEC_SEQ, ATTN_W), _BF16),
        scratch_shapes=[pltpu.VMEM((4, PAIR, _GQA_KEYS), _BF16), pltpu.VMEM((2, PAIR, _GQA_KEYS), _BF16),
                        pltpu.VMEM((1, _GQA_UNIT, _GQA_KEYS), _F32), pltpu.VMEM((1, _GQA_UNIT, _GQA_KEYS), _F32),
                        pltpu.VMEM((1, _GQA_UNIT, _GQA_KEYS), _F32),
                        pltpu.VMEM((1, _GQA_UNIT, _GQA_KEYS), _BF16), pltpu.VMEM((1, _GQA_UNIT, _GQA_KEYS), _BF16)],
        compiler_params=_params(("arbitrary", "arbitrary"), 48),
        name="sample_gqa",
    )(q, k_t, v_t, cache_k_t, cache_v_t)


_NA_GROUP = 4
_NA_GROUPS = GRID_H // _NA_GROUP
_NA_GROUPS_PER_STEP = 4
_NA_WIN_ROWS = 12
_NA_Q = _NA_GROUP * GRID_W
_NA_KEYS = _NA_WIN_ROWS * GRID_W
_NA_WIN_BLOCKS = _NA_KEYS // KV_BLK
_NA_VARIANTS = 3
_NA_OFFSETS = 2 * NA_WIN_R - 1
_NA_RPB_ROWS = 16

assert _NA_GROUP * GRID_W == KV_BLK


def _na_window_start(group):
    lo, hi = 0, GRID_H - _NA_WIN_ROWS
    start = _NA_GROUP * group - NA_WIN_R // 2
    if isinstance(group, int):
        return min(max(start, lo), hi)
    return jnp.clip(start, lo, hi)


def _na_row_plan():
    plan = []
    for group in (0, 1, _NA_GROUPS - 1):
        w0 = _na_window_start(group)
        rows = []
        for i in range(_NA_GROUP):
            r = _NA_GROUP * group + i
            rs = min(max(r - NA_WIN_R // 2, 0), GRID_H - NA_WIN_R)
            rows.append([kr - r + NA_WIN_R - 1 if rs <= kr < rs + NA_WIN_R else None
                         for kr in range(w0, w0 + _NA_WIN_ROWS)])
        plan.append(rows)
    return plan


def _na_build_bias(rpb_ref, bias_ref, t2_ref):
    shape = (GRID_W, LANES)
    c = lax.broadcasted_iota(jnp.int32, shape, 0)
    lane = lax.broadcasted_iota(jnp.int32, shape, 1)
    kc = lane % GRID_W
    cs = jnp.clip(c - NA_WIN_C // 2, 0, GRID_W - NA_WIN_C)
    col_ok = (kc >= cs) & (kc < cs + NA_WIN_C)
    low = lane < GRID_W
    neg = jnp.full(shape, NEG_INF, _F32)
    plan = _na_row_plan()

    def head_body(h, carry):
        for d in range(_NA_OFFSETS):
            v = jnp.broadcast_to(rpb_ref[h, d:d + 1, :], shape)
            lo = pltpu.roll(v, LANES - (NA_WIN_C - 1), 1, stride=1, stride_axis=0)
            hi = pltpu.roll(v, GRID_W - (NA_WIN_C - 1), 1, stride=1, stride_axis=0)
            t2_ref[d] = jnp.where(col_ok, jnp.where(low, lo, hi) * LOG2E, neg)
        for variant in range(_NA_VARIANTS):
            for i in range(_NA_GROUP):
                for m in range(_NA_WIN_ROWS // 2):
                    d_even, d_odd = plan[variant][i][2 * m], plan[variant][i][2 * m + 1]
                    even = neg if d_even is None else t2_ref[d_even]
                    odd = neg if d_odd is None else t2_ref[d_odd]
                    tile = neg if d_even is None and d_odd is None else jnp.where(low, even, odd)
                    bias_ref[variant, h, i * GRID_W:(i + 1) * GRID_W, m * LANES:(m + 1) * LANES] = tile
        return carry

    lax.fori_loop(0, NA_HEADS, head_body, 0)


def _na_kernel(q_ref, k_ref, v_ref, kc_ref, vc_ref, rpb_ref, o_ref, bias_ref, t2_ref,
               s0_ref, s1_ref, s2_ref, p0_ref, p1_ref):
    step = pl.program_id(1)

    @pl.when((pl.program_id(0) == 0) & (step == 0))
    def _():
        _na_build_bias(rpb_ref, bias_ref, t2_ref)

    low_out = _low_half((_NA_Q, LANES))
    s_refs = _staging_views(s0_ref, s1_ref, s2_ref)
    p_refs = _staging_views(p0_ref, p1_ref)
    denoms, outs = {}, {}
    units = [(g, h) for g in range(_NA_GROUPS_PER_STEP) for h in range(NA_HEADS)]

    def geometry(g):
        group = step * _NA_GROUPS_PER_STEP + g
        variant = jnp.where(group == 0, 0, jnp.where(group == _NA_GROUPS - 1, 2, 1))
        return variant, _na_window_start(group) // _NA_GROUP

    geo = [geometry(g) for g in range(_NA_GROUPS_PER_STEP)]

    def scores(u):
        g, h = units[u]
        variant, blk0 = geo[g]
        p, parity = divmod(h, 2)
        q = q_ref[g * _NA_Q:(g + 1) * _NA_Q, p * LANES:(p + 1) * LANES]
        s_ref = s_refs[u % len(s_refs)]
        for w in range(_NA_WIN_BLOCKS):
            lanes = slice(w * KV_BLK, (w + 1) * KV_BLK)
            k = _head_rows(_pair_block(k_ref, (blk0 + w,), p), parity)
            s_ref[:, lanes] = _dot(q, k) + bias_ref[variant, h, :, lanes]
        kc = _head_rows(_pair_block(kc_ref, (), p).astype(_BF16), parity)
        s_ref[:, _NA_KEYS:_NA_KEYS + PAST_LEN] = _dot(q, kc)

    def softmax(u):
        denoms[u] = _softmax_to(s_refs[u % len(s_refs)], p_refs[u % 2])

    def weighted_values(u):
        g, h = units[u]
        _, blk0 = geo[g]
        p = h // 2
        p_ref = p_refs[u % 2]
        pv = _dot_nt(p_ref[:, _NA_KEYS:_NA_KEYS + PAST_LEN], _pair_block(vc_ref, (), p).astype(_BF16))
        for w in range(_NA_WIN_BLOCKS):
            pv = pv + _dot_nt(p_ref[:, w * KV_BLK:(w + 1) * KV_BLK], _pair_block(v_ref, (blk0 + w,), p))
        outs[u] = pv / denoms[u]
        if h % 2 == 1:
            o_ref[g * _NA_Q:(g + 1) * _NA_Q, p * LANES:(p + 1) * LANES] = (
                jnp.where(low_out, outs[u - 1], outs[u]).astype(o_ref.dtype))

    _run_pipeline(len(units), scores, softmax, weighted_values, len(s_refs))


def _sample_na(q, k_t, v_t, cache_k_t, cache_v_t, rpb, layer):
    steps = _NA_GROUPS // _NA_GROUPS_PER_STEP
    tq = _NA_GROUPS_PER_STEP * _NA_Q
    new_spec = pl.BlockSpec((KV_BLOCKS, NA_HEADS, HEAD_DIM, KV_BLK), lambda b, i: (b, 0, 0, 0))
    cache_spec = pl.BlockSpec((None, None, NA_HEADS, HEAD_DIM, PAST_LEN), lambda b, i: (b, layer, 0, 0, 0))
    return pl.pallas_call(
        _na_kernel,
        grid=(DEC_BATCH, steps),
        in_specs=[pl.BlockSpec((tq, ATTN_W), lambda b, i: (b * steps + i, 0)),
                  new_spec, new_spec, cache_spec, cache_spec,
                  pl.BlockSpec((None, NA_HEADS, _NA_RPB_ROWS, LANES), lambda b, i: (layer, 0, 0, 0))],
        out_specs=pl.BlockSpec((tq, ATTN_W), lambda b, i: (b * steps + i, 0)),
        out_shape=jax.ShapeDtypeStruct((DEC_BATCH * DEC_SEQ, ATTN_W), _BF16),
        scratch_shapes=[pltpu.VMEM((_NA_VARIANTS, NA_HEADS, _NA_Q, _NA_KEYS), _F32),
                        pltpu.VMEM((_NA_OFFSETS, GRID_W, LANES), _F32),
                        pltpu.VMEM((1, _NA_Q, _NA_KEYS + PAST_LEN), _F32),
                        pltpu.VMEM((1, _NA_Q, _NA_KEYS + PAST_LEN), _F32),
                        pltpu.VMEM((1, _NA_Q, _NA_KEYS + PAST_LEN), _F32),
                        pltpu.VMEM((1, _NA_Q, _NA_KEYS + PAST_LEN), _BF16),
                        pltpu.VMEM((1, _NA_Q, _NA_KEYS + PAST_LEN), _BF16)],
        compiler_params=_params(("arbitrary", "arbitrary"), 48),
        name="sample_na",
    )(q, k_t, v_t, cache_k_t, cache_v_t, rpb)


def _out_mlp_kernel(*refs, seg_len, halo):
    if halo:
        h_ref, hprev_ref, hnext_ref = refs[:3]
        refs = refs[3:]
    else:
        h_ref = refs[0]
        refs = refs[1:]
    (b_ref, c_ref, x_ref, g1_ref, sh2_ref, sc2_ref, g2_ref, n2_ref, wo_ref, w1_ref, w2_ref,
     cw_ref, cb_ref, cg_ref, cbeta_ref, o_ref, hp_ref, a_ref) = refs
    b0 = CONV_CH
    c0 = CONV_CH + ATTN_W
    tm = x_ref.shape[0]

    seg_stride = seg_len + 2 * _CONV_PAD
    zeros = jnp.zeros((_CONV_PAD, CONV_CH), _F32)
    if halo:
        first_half = pl.program_id(0) % 2 == 0
        before = jnp.where(first_half, zeros, hprev_ref[...].astype(_F32))
        after = jnp.where(first_half, hnext_ref[...].astype(_F32), zeros)
    else:
        before = after = zeros
    for seg in range(tm // seg_len):
        base = seg * seg_stride
        hp_ref[base:base + _CONV_PAD, :] = before
        hp_ref[base + _CONV_PAD:base + _CONV_PAD + seg_len, :] = (
            h_ref[seg * seg_len:(seg + 1) * seg_len, :].astype(_F32))
        hp_ref[base + _CONV_PAD + seg_len:base + seg_stride, :] = after

    def conv_module(rows):
        for r0 in range(rows.start, rows.stop, _CONV_ROWS):
            t0 = (r0 // seg_len) * seg_stride + r0 % seg_len
            y = _conv_chunk(hp_ref, t0, cw_ref, cb_ref, cg_ref, cbeta_ref)
            a_ref[r0:r0 + _CONV_ROWS, :] = y.astype(a_ref.dtype)

    def attention_residual(rows):
        conv_module(rows)
        y = (_dot(a_ref[rows, :], wo_ref[0:b0, :]) + _dot(b_ref[rows, :], wo_ref[b0:c0, :])
             + _dot(c_ref[rows, :], wo_ref[c0:D_MODEL, :]))
        x1 = x_ref[rows, :] + g1_ref[...] * y
        ms = jnp.mean(x1 * x1, axis=-1, keepdims=True)
        h = (x1 * lax.rsqrt(ms + EPS)) * n2_ref[...]
        return x1, (h * (1.0 + sc2_ref[...]) + sh2_ref[...]).astype(_BF16)

    n_sub = tm // _SUB_TILE
    sub_rows = [slice(s * _SUB_TILE, (s + 1) * _SUB_TILE) for s in range(n_sub)]
    firsts = [attention_residual(rows) for rows in sub_rows]
    for rows, (x1, hb) in zip(sub_rows, firsts):
        acc = jnp.zeros(x1.shape, _F32)
        for j in range(D_FF // FF_CHUNK):
            t = jnp.maximum(_dot(hb, w1_ref[:, j * FF_CHUNK:(j + 1) * FF_CHUNK]), 0.0)
            acc = acc + _dot((t * t).astype(_BF16), w2_ref[j * FF_CHUNK:(j + 1) * FF_CHUNK, :])
        o_ref[rows, :] = x1 + g2_ref[...] * acc


def _out_mlp(h, b, c, x, mod4, layer, norm2_g, w_out, w1, w2, dw_w, dw_b, ln_g, ln_b, *, sample):
    tokens = x.shape[0]
    tm = TM
    row = _row_fn(sample, tm)
    seq = DEC_SEQ if sample else SEQ
    seg_len = min(seq, tm)
    halo = seq > tm
    assert tm % seg_len == 0 and (not halo or seq == 2 * tm)

    def tok(w):
        return pl.BlockSpec((tm, w), lambda i: (i, 0))

    def resident(shape):
        return pl.BlockSpec((None,) + shape, lambda i: (layer,) + (0,) * len(shape),
                            pipeline_mode=pl.Buffered(1))

    def vec(w):
        return pl.BlockSpec((None, 1, w), lambda i: (layer, 0, 0))

    in_specs = [tok(CONV_CH)]
    args = [h]
    if halo:
        per_tile = tm // _CONV_PAD
        last = tokens // _CONV_PAD - 1
        in_specs += [pl.BlockSpec((_CONV_PAD, CONV_CH), lambda i: (jnp.maximum(i * per_tile - 1, 0), 0)),
                     pl.BlockSpec((_CONV_PAD, CONV_CH), lambda i: (jnp.minimum((i + 1) * per_tile, last), 0))]
        args += [h, h]
    in_specs += [tok(ATTN_W), tok(ATTN_W), tok(D_MODEL),
                 _mod_spec(layer, row, 2), _mod_spec(layer, row, 3),
                 _mod_spec(layer, row, 4), _mod_spec(layer, row, 5),
                 vec(D_MODEL),
                 resident((D_MODEL, D_MODEL)), resident((D_MODEL, D_FF)), resident((D_FF, D_MODEL)),
                 pl.BlockSpec((None, CONV_K, CONV_CH), lambda i: (layer, 0, 0)),
                 vec(CONV_CH), vec(CONV_CH), vec(CONV_CH)]
    args += [b, c, x, mod4, mod4, mod4, mod4, norm2_g, w_out, w1, w2, dw_w, dw_b, ln_g, ln_b]
    return pl.pallas_call(
        functools.partial(_out_mlp_kernel, seg_len=seg_len, halo=halo),
        grid=(tokens // tm,),
        in_specs=in_specs,
        out_specs=tok(D_MODEL),
        out_shape=jax.ShapeDtypeStruct((tokens, D_MODEL), _F32),
        scratch_shapes=[pltpu.VMEM(((tm // seg_len) * (seg_len + 2 * _CONV_PAD), CONV_CH), _F32),
                        pltpu.VMEM((tm, CONV_CH), _BF16)],
        compiler_params=_params(("arbitrary",), 56),
        name="out_mlp_sample" if sample else "out_mlp_prompt",
    )(*args)


def kernel(x_prompt, x_sample, cache_attn_k, cache_attn_v, cache_na_k, cache_na_v, c, c_ctx, ada_w, ada_b, norm1_g, norm2_g, w_in, conv_dw_w, conv_dw_b, conv_ln_g, conv_ln_b, attn_q_g, attn_k_g, na_q_g, na_k_g, na_rpb, w_out, mlp_w1, mlp_w2):
    cvec = jnp.concatenate(
        [c_ctx[None, :], c, jnp.zeros((MOD_ROWS - 1 - DEC_BATCH, D_MODEL), _F32)], axis=0)
    mod4 = _modulation(cvec, ada_w, ada_b).reshape(DEPTH, MOD_ROWS, 1, 6 * D_MODEL)

    def heads(g, n):
        return jnp.tile(g, (1, n))
    gains = jnp.concatenate(
        [heads(attn_q_g, GQA_HEADS) * Q_SCALE, heads(attn_k_g, GQA_KV_HEADS),
         heads(na_q_g, NA_HEADS) * Q_SCALE, heads(na_k_g, NA_HEADS)], axis=1)[:, None, :]
    rope_tabs = _rope_tables()
    rpb = jnp.pad(na_rpb, ((0, 0), (0, 0), (0, _NA_RPB_ROWS - _NA_OFFSETS),
                           (0, LANES - (2 * NA_WIN_C - 1))))

    w_in_b = w_in.astype(_BF16)
    w_out_b = w_out.astype(_BF16)
    w1_b = mlp_w1.astype(_BF16)
    w2_b = mlp_w2.astype(_BF16)
    n1 = norm1_g.reshape(DEPTH, 1, D_MODEL)
    n2 = norm2_g.reshape(DEPTH, 1, D_MODEL)
    dw_b = conv_dw_b.reshape(DEPTH, 1, CONV_CH)
    ln_g = conv_ln_g.reshape(DEPTH, 1, CONV_CH)
    ln_b = conv_ln_b.reshape(DEPTH, 1, CONV_CH)
    to_t = (0, 1, 3, 4, 2)
    ck_a, cv_a, ck_n, cv_n = (t.transpose(to_t) for t in (cache_attn_k, cache_attn_v, cache_na_k, cache_na_v))

    xp = x_prompt.reshape(BATCH * SEQ, D_MODEL)
    xs = x_sample.reshape(DEC_BATCH * DEC_SEQ, D_MODEL)
    ctx = tuple(jnp.zeros((BATCH, DEPTH, nh, HEAD_DIM, SEQ), _F32)
                for nh in (GQA_KV_HEADS, GQA_KV_HEADS, NA_HEADS, NA_HEADS))
    for layer in range(DEPTH):
        u, qa, ka, va, qn, kn, vn = _in_proj(xp, mod4, layer, n1, w_in_b, gains, None, ctx, sample=False)
        ctx = (ka, va, kn, vn)
        b_out, c_out = _prompt_attention(qa, ka, va, qn, kn, vn, layer)
        xp = _out_mlp(u, b_out, c_out, xp, mod4, layer, n2, w_out_b, w1_b, w2_b,
                      conv_dw_w, dw_b, ln_g, ln_b, sample=False)
        u, qa, ka, va, qn, kn, vn = _in_proj(xs, mod4, layer, n1, w_in_b, gains, rope_tabs, None, sample=True)
        b_out = _sample_gqa(qa, ka, va, ck_a, cv_a, layer)
        c_out = _sample_na(qn, kn, vn, ck_n, cv_n, rpb, layer)
        xs = _out_mlp(u, b_out, c_out, xs, mod4, layer, n2, w_out_b, w1_b, w2_b,
                      conv_dw_w, dw_b, ln_g, ln_b, sample=True)

    outs = [t.transpose(0, 1, 4, 2, 3) for t in ctx]
    return (xp.reshape(BATCH, SEQ, D_MODEL), xs.reshape(DEC_BATCH, DEC_SEQ, D_MODEL), *outs)
```

```python
import functools
import math

import numpy as np
import jax
import jax.numpy as jnp
from jax import lax
from jax.experimental import pallas as pl
from jax.experimental.pallas import tpu as pltpu

D_MODEL = 1024
BATCH = 32
SEQ = 256
DEPTH = 4
DEC_BATCH = 8
DEC_SEQ = 2048
PAST_LEN = 256
GRID_W = 64
GRID_H = DEC_SEQ // GRID_W
HEAD_DIM = 64
CONV_CH = D_MODEL // 4
CONV_K = 31
ATTN_W = (D_MODEL - CONV_CH) // 2
GQA_HEADS = ATTN_W // HEAD_DIM
GQA_KV_HEADS = GQA_HEADS // 3
GQA_GROUP = GQA_HEADS // GQA_KV_HEADS
NA_HEADS = ATTN_W // HEAD_DIM
D_FF = 4 * D_MODEL
NA_WIN_R = 8
NA_WIN_C = 16
ROPE_THETA = 10000.0
EPS = 1e-6
NEG_INF = -1e30

KV_W = GQA_KV_HEADS * HEAD_DIM
IN_W = 2 * CONV_CH + ATTN_W + 2 * KV_W + 3 * ATTN_W
_U0, _QA0, _KA0, _VA0, _QN0, _KN0, _VN0 = 0, 512, 896, 1024, 1152, 1536, 1920
LANES = 128
SUBLANES = 8
MXU_W = 256
MOD_ROWS = 16
LOG2E = math.log2(math.e)
Q_SCALE = HEAD_DIM ** -0.5 * LOG2E
_SUB_TILE = 512
TM = 2 * _SUB_TILE
FF_CHUNK = 512
KV_BLK = 256
KV_BLOCKS = DEC_SEQ // KV_BLK
PAIR = 2 * HEAD_DIM
MIB = 1024 * 1024

assert SEQ == KV_BLK and PAST_LEN == KV_BLK and _SUB_TILE % KV_BLK == 0 and PAIR == LANES
assert DEC_SEQ % TM == 0 and (BATCH * SEQ) % TM == 0

_F32 = jnp.float32
_BF16 = jnp.bfloat16


def _params(semantics, vmem_mib):
    return pltpu.CompilerParams(dimension_semantics=semantics, vmem_limit_bytes=vmem_mib * MIB)


def _dot(a, b):
    return jnp.dot(a, b, preferred_element_type=_F32)


def _dot_nt(a, b):
    return lax.dot_general(a, b, (((1,), (1,)), ((), ())), preferred_element_type=_F32)


def _low_half(shape):
    return lax.broadcasted_iota(jnp.int32, shape, len(shape) - 1) % LANES < HEAD_DIM


def _mod_kernel(c_ref, w_ref, b_ref, o_ref):
    c = c_ref[...]
    s = (c * jax.nn.sigmoid(c)).astype(_BF16)
    o_ref[...] = _dot(s, w_ref[...].astype(_BF16)) + b_ref[...]


def _modulation(cvec, ada_w, ada_b):
    tn = 1536
    return pl.pallas_call(
        _mod_kernel,
        grid=(DEPTH, 6 * D_MODEL // tn),
        in_specs=[
            pl.BlockSpec((MOD_ROWS, D_MODEL), lambda l, j: (0, 0)),
            pl.BlockSpec((None, D_MODEL, tn), lambda l, j: (l, 0, j)),
            pl.BlockSpec((None, 1, tn), lambda l, j: (l, 0, j)),
        ],
        out_specs=pl.BlockSpec((None, MOD_ROWS, tn), lambda l, j: (l, 0, j)),
        out_shape=jax.ShapeDtypeStruct((DEPTH, MOD_ROWS, 6 * D_MODEL), _F32),
        compiler_params=_params(("arbitrary", "arbitrary"), 40),
        name="modulation",
    )(cvec, ada_w, ada_b.reshape(DEPTH, 1, 6 * D_MODEL))


def _mod_spec(layer, row_fn, part):
    return pl.BlockSpec((None, None, 1, D_MODEL), lambda i: (layer, row_fn(i), 0, part))


def _row_fn(sample, tm):
    tiles_per_seq = DEC_SEQ // tm
    if sample:
        return lambda i: 1 + i // tiles_per_seq
    return lambda i: 0


def _rope_tables():
    t = np.arange(DEC_SEQ)
    pos = (t // GRID_W, t % GRID_W)
    half = HEAD_DIM // 2
    quarter = half // 2
    inv = 1.0 / (ROPE_THETA ** (np.arange(quarter) * 2.0 / half))
    lane = np.arange(LANES)
    d = lane % HEAD_DIM
    axis = d // half
    second = (d % half) >= quarter
    p = np.where(axis[None, :] == 0, pos[0][:, None], pos[1][:, None]).astype(np.float64)
    ang = p * inv[d % quarter][None, :]
    cos, sin = np.cos(ang), np.sin(ang)
    sin_next = np.where(second[None, :], 0.0, -sin)
    sin_prev = np.where(second[None, :], sin, 0.0)
    return tuple(jnp.asarray(a, _F32) for a in (cos, sin_next, sin_prev))


_GAIN_W = 2 * ATTN_W + KV_W + ATTN_W


def _in_proj_kernel(*refs, rope, n_alias):
    n_in = 9 if rope else 6
    if rope:
        x_ref, sh_ref, sc_ref, n1_ref, w_ref, g_ref, cos_ref, sn_ref, sp_ref = refs[:n_in]
    else:
        x_ref, sh_ref, sc_ref, n1_ref, w_ref, g_ref = refs[:n_in]
    u_ref, qa_ref, ka_ref, va_ref, qn_ref, kn_ref, vn_ref = refs[n_in + n_alias:]

    r = lax.broadcasted_iota(jnp.int32, (MXU_W, MXU_W), 0) // HEAD_DIM
    c = lax.broadcasted_iota(jnp.int32, (MXU_W, MXU_W), 1) // HEAD_DIM
    seg_mean = jnp.where(r == c, 1.0 / HEAD_DIM, 0.0).astype(_BF16)

    def normed_input(rows):
        x = x_ref[rows, :]
        ms = jnp.mean(x * x, axis=-1, keepdims=True)
        h = (x * lax.rsqrt(ms + EPS)) * n1_ref[...]
        return (h * (1.0 + sc_ref[...]) + sh_ref[...]).astype(_BF16)

    def project(sub, hb):
        rows = slice(sub * _SUB_TILE, (sub + 1) * _SUB_TILE)
        blk0 = sub * (_SUB_TILE // KV_BLK)

        y_all = _dot(hb, w_ref[...])

        def proj(c0, width):
            return y_all[:, c0:c0 + width]

        def rotate(blk):
            return (blk * cos_ref[rows, :] + pltpu.roll(blk, LANES - 16, 1) * sn_ref[rows, :]
                    + pltpu.roll(blk, 16, 1) * sp_ref[rows, :])

        def head_norm(c0, width, gain0, rotary):
            for b in range(width // MXU_W):
                y = proj(c0 + b * MXU_W, MXU_W)
                ss = _dot((y * y).astype(_BF16), seg_mean)
                g0 = gain0 + b * MXU_W
                y = y * lax.rsqrt(ss + EPS) * g_ref[:, g0:g0 + MXU_W]
                for j in range(MXU_W // LANES):
                    blk = y[:, j * LANES:(j + 1) * LANES]
                    yield (b * MXU_W) // LANES + j, rotate(blk) if rotary else blk

        def store_transposed(ref, pair, blk):
            t = blk.T
            for wblk in range(_SUB_TILE // KV_BLK):
                piece = t[:, wblk * KV_BLK:(wblk + 1) * KV_BLK].reshape(2, HEAD_DIM, KV_BLK)
                ref[blk0 + wblk, 2 * pair:2 * pair + 2] = piece.astype(ref.dtype)

        u = proj(_U0, 2 * CONV_CH)
        u_ref[rows, :] = (u[:, 0:CONV_CH] * jax.nn.sigmoid(u[:, CONV_CH:2 * CONV_CH])).astype(u_ref.dtype)
        q_blocks = ATTN_W // LANES
        for j, blk in head_norm(_QA0, ATTN_W + KV_W, 0, rope):
            if j < q_blocks:
                qa_ref[rows, j * LANES:(j + 1) * LANES] = blk.astype(qa_ref.dtype)
            else:
                store_transposed(ka_ref, 0, blk)
        store_transposed(va_ref, 0, proj(_VA0, KV_W))
        for j, blk in head_norm(_QN0, 2 * ATTN_W, ATTN_W + KV_W, False):
            if j < q_blocks:
                qn_ref[rows, j * LANES:(j + 1) * LANES] = blk.astype(qn_ref.dtype)
            else:
                store_transposed(kn_ref, j - q_blocks, blk)
        vn = proj(_VN0, ATTN_W)
        for j in range(ATTN_W // LANES):
            store_transposed(vn_ref, j, vn[:, j * LANES:(j + 1) * LANES])

    n_sub = x_ref.shape[0] // _SUB_TILE
    inputs = [normed_input(slice(s * _SUB_TILE, (s + 1) * _SUB_TILE)) for s in range(n_sub)]
    for s in range(n_sub):
        project(s, inputs[s])


def _in_proj(x, mod4, layer, norm1_g, w_in, gains, rope_tabs, prev_ctx, *, sample):
    tokens = x.shape[0]
    tm = TM
    blocks = tm // KV_BLK
    row = _row_fn(sample, tm)
    in_specs = [
        pl.BlockSpec((tm, D_MODEL), lambda i: (i, 0)),
        _mod_spec(layer, row, 0),
        _mod_spec(layer, row, 1),
        pl.BlockSpec((None, 1, D_MODEL), lambda i: (layer, 0, 0)),
        pl.BlockSpec((None, D_MODEL, IN_W), lambda i: (layer, 0, 0), pipeline_mode=pl.Buffered(1)),
        pl.BlockSpec((None, 1, _GAIN_W), lambda i: (layer, 0, 0)),
    ]
    args = [x, mod4, mod4, norm1_g, w_in, gains]
    aliases = {}
    kv_heads = (GQA_KV_HEADS, GQA_KV_HEADS, NA_HEADS, NA_HEADS)
    if sample:
        tiles_per_seq = DEC_SEQ // tm
        in_specs += [pl.BlockSpec((tm, LANES), lambda i: (i % tiles_per_seq, 0))] * 3
        args += list(rope_tabs)
        kv_specs = [pl.BlockSpec((blocks, nh, HEAD_DIM, KV_BLK), lambda i: (i, 0, 0, 0)) for nh in kv_heads]
        kv_shapes = [jax.ShapeDtypeStruct((tokens // KV_BLK, nh, HEAD_DIM, KV_BLK), _BF16) for nh in kv_heads]
    else:
        kv_specs = [pl.BlockSpec((blocks, None, nh, HEAD_DIM, KV_BLK), lambda i: (i, layer, 0, 0, 0))
                    for nh in kv_heads]
        kv_shapes = [jax.ShapeDtypeStruct((BATCH, DEPTH, nh, HEAD_DIM, SEQ), _F32) for nh in kv_heads]
        kv_out_index = (2, 3, 5, 6)
        for arr, out_idx in zip(prev_ctx, kv_out_index):
            aliases[len(args)] = out_idx
            in_specs.append(pl.BlockSpec(memory_space=pl.ANY))
            args.append(arr)

    def tok(w):
        return pl.BlockSpec((tm, w), lambda i: (i, 0))

    def tok_shape(w):
        return jax.ShapeDtypeStruct((tokens, w), _BF16)

    out_specs = [tok(CONV_CH), tok(ATTN_W), kv_specs[0], kv_specs[1], tok(ATTN_W), kv_specs[2], kv_specs[3]]
    out_shape = [tok_shape(CONV_CH), tok_shape(ATTN_W), kv_shapes[0], kv_shapes[1],
                 tok_shape(ATTN_W), kv_shapes[2], kv_shapes[3]]
    return pl.pallas_call(
        functools.partial(_in_proj_kernel, rope=sample, n_alias=len(aliases)),
        grid=(tokens // tm,),
        in_specs=in_specs,
        out_specs=out_specs,
        out_shape=out_shape,
        input_output_aliases=aliases,
        compiler_params=_params(("arbitrary",), 48),
        name="in_proj_sample" if sample else "in_proj_prompt",
    )(*args)


_CONV_ROWS = 64
_CONV_PAD = 16


def _conv_chunk(hp_ref, t0, w_ref, b_ref, g_ref, beta_ref):
    first = _CONV_PAD - CONV_K // 2
    span = _CONV_ROWS + SUBLANES
    halves = []
    for c0 in range(0, CONV_CH, LANES):
        lanes = slice(c0, c0 + LANES)
        acc = jnp.zeros((_CONV_ROWS, LANES), _F32) + b_ref[:, lanes]
        for s in range(SUBLANES):
            part = None
            for a in range((first + CONV_K + SUBLANES - 1) // SUBLANES):
                k = SUBLANES * a + s - first
                if 0 <= k < CONV_K:
                    term = hp_ref[pl.ds(t0 + SUBLANES * a, span), lanes] * w_ref[k:k + 1, lanes]
                    part = term if part is None else part + term
            acc = acc + part[s:s + _CONV_ROWS, :]
        halves.append(acc)
    acc = jnp.concatenate(halves, axis=1)
    mu = jnp.mean(acc, axis=-1, keepdims=True)
    cen = acc - mu
    var = jnp.mean(cen * cen, axis=-1, keepdims=True)
    y = cen * lax.rsqrt(var + EPS) * g_ref[...] + beta_ref[...]
    return y * jax.nn.sigmoid(y)


_SOFTMAX_ROWS = 16
_SOFTMAX_LANES = 768


def _softmax_to(s_ref, p_ref):
    n, width = s_ref.shape
    sums = []
    for r0 in range(0, n, _SOFTMAX_ROWS):
        rows = slice(r0, r0 + _SOFTMAX_ROWS)
        m = jnp.max(s_ref[rows, :], axis=-1, keepdims=True)
        part = None
        for l0 in range(0, width, _SOFTMAX_LANES):
            l1 = min(l0 + _SOFTMAX_LANES, width)
            e = jnp.exp2(s_ref[rows, l0:l1] - m)
            p_ref[rows, l0:l1] = e.astype(p_ref.dtype)
            for j in range(0, l1 - l0, LANES):
                part = e[:, j:j + LANES] if part is None else part + e[:, j:j + LANES]
        sums.append(jnp.sum(part, axis=-1, keepdims=True))
    return jnp.concatenate(sums, axis=0)


def _staging_views(*refs):
    zero = jnp.minimum(pl.program_id(0), 0)
    return tuple(ref.at[zero] for ref in refs)


def _run_pipeline(n, scores, softmax, weighted_values, score_buffers):
    scores(0)
    softmax(0)
    for u in range(1, min(score_buffers, n)):
        scores(u)
    for u in range(n):
        if u + 1 < n:
            softmax(u + 1)
        weighted_values(u)
        if u + score_buffers < n:
            scores(u + score_buffers)


def _head_rows(block, parity):
    rows = lax.broadcasted_iota(jnp.int32, block.shape, 0)
    keep = rows < HEAD_DIM if parity == 0 else rows >= HEAD_DIM
    return jnp.where(keep, block, jnp.zeros_like(block))


def _pair_block(ref, idx, pair):
    return ref[idx + (slice(2 * pair, 2 * pair + 2),)].reshape(PAIR, KV_BLK)


def _softmax_pv_t(s, value_t):
    m = jnp.max(s, axis=-1, keepdims=True)
    e = jnp.exp2(s - m)
    return _dot_nt(e.astype(_BF16), value_t) / jnp.sum(e, axis=-1, keepdims=True)


_PROMPT_SEQS = 2


def _prompt_attn_kernel(qa_ref, ka_ref, va_ref, qn_ref, kn_ref, vn_ref, b_ref, c_ref):
    low_out = _low_half((SEQ, LANES))
    zero = jnp.zeros((HEAD_DIM, SEQ), _BF16)
    for p in range(GQA_HEADS // 2):
        cols = slice(p * LANES, (p + 1) * LANES)
        for b in range(_PROMPT_SEQS):
            rows = slice(b * SEQ, (b + 1) * SEQ)
            q = qa_ref[rows, cols]
            outs = []
            for parity in range(2):
                kv = (2 * p + parity) // GQA_GROUP
                k = ka_ref[b, kv].astype(_BF16)
                v = va_ref[b, kv].astype(_BF16)
                k_sel = jnp.concatenate([k, zero] if parity == 0 else [zero, k], axis=0)
                outs.append(_softmax_pv_t(_dot(q, k_sel), jnp.concatenate([v, v], axis=0)))
            b_ref[rows, cols] = jnp.where(low_out, outs[0], outs[1]).astype(b_ref.dtype)
            q = qn_ref[rows, cols]
            k = _pair_block(kn_ref, (b,), p).astype(_BF16)
            v = _pair_block(vn_ref, (b,), p).astype(_BF16)
            outs = [_softmax_pv_t(_dot(q, _head_rows(k, parity)), v) for parity in range(2)]
            c_ref[rows, cols] = jnp.where(low_out, outs[0], outs[1]).astype(c_ref.dtype)


def _prompt_attention(qa, ka_t, va_t, qn, kn_t, vn_t, layer):
    n = _PROMPT_SEQS

    def tok(w):
        return pl.BlockSpec((n * SEQ, w), lambda i: (i, 0))

    def ctx(nh):
        return pl.BlockSpec((n, None, nh, HEAD_DIM, SEQ), lambda i: (i, layer, 0, 0, 0))

    return pl.pallas_call(
        _prompt_attn_kernel,
        grid=(BATCH // n,),
        in_specs=[tok(ATTN_W), ctx(GQA_KV_HEADS), ctx(GQA_KV_HEADS), tok(ATTN_W), ctx(NA_HEADS), ctx(NA_HEADS)],
        out_specs=[tok(ATTN_W), tok(ATTN_W)],
        out_shape=[jax.ShapeDtypeStruct((BATCH * SEQ, ATTN_W), _BF16)] * 2,
        compiler_params=_params(("arbitrary",), 32),
        name="prompt_attention",
    )(qa, ka_t, va_t, qn, kn_t, vn_t)


_GQA_TQ = 512
_GQA_UNIT = 256
_GQA_KEYS = DEC_SEQ + PAST_LEN


def _gqa_kernel(q_ref, k_ref, v_ref, kc_ref, vc_ref, o_ref, ksel_ref, vdup_ref,
                s0_ref, s1_ref, s2_ref, p0_ref, p1_ref):
    @pl.when(pl.program_id(1) == 0)
    def _():
        ksel_ref[...] = jnp.zeros(ksel_ref.shape, _BF16)
        for blk in range(KV_BLOCKS + 1):
            lanes = slice(blk * KV_BLK, (blk + 1) * KV_BLK)
            for kv in range(GQA_KV_HEADS):
                if blk < KV_BLOCKS:
                    k, v = k_ref[blk, kv], v_ref[blk, kv]
                else:
                    k, v = kc_ref[kv].astype(_BF16), vc_ref[kv].astype(_BF16)
                ksel_ref[2 * kv, 0:HEAD_DIM, lanes] = k
                ksel_ref[2 * kv + 1, HEAD_DIM:PAIR, lanes] = k
                vdup_ref[kv, 0:HEAD_DIM, lanes] = v
                vdup_ref[kv, HEAD_DIM:PAIR, lanes] = v

    low_out = _low_half((_GQA_UNIT, LANES))
    s_refs = _staging_views(s0_ref, s1_ref, s2_ref)
    p_refs = _staging_views(p0_ref, p1_ref)
    denoms, outs = {}, {}
    units = [(g, h) for g in range(_GQA_TQ // _GQA_UNIT) for h in range(GQA_HEADS)]

    def scores(u):
        g, head = units[u]
        p = head // 2
        kid = 2 * (head // GQA_GROUP) + head % 2
        q = q_ref[g * _GQA_UNIT:(g + 1) * _GQA_UNIT, p * LANES:(p + 1) * LANES]
        s_refs[u % len(s_refs)][...] = _dot(q, ksel_ref[kid])

    def softmax(u):
        denoms[u] = _softmax_to(s_refs[u % len(s_refs)], p_refs[u % 2])

    def weighted_values(u):
        g, head = units[u]
        outs[u] = _dot_nt(p_refs[u % 2][...], vdup_ref[head // GQA_GROUP]) / denoms[u]
        if head % 2 == 1:
            p = head // 2
            o_ref[g * _GQA_UNIT:(g + 1) * _GQA_UNIT, p * LANES:(p + 1) * LANES] = (
                jnp.where(low_out, outs[u - 1], outs[u]).astype(o_ref.dtype))

    _run_pipeline(len(units), scores, softmax, weighted_values, len(s_refs))


def _sample_gqa(q, k_t, v_t, cache_k_t, cache_v_t, layer):
    tiles = DEC_SEQ // _GQA_TQ
    new_spec = pl.BlockSpec((KV_BLOCKS, GQA_KV_HEADS, HEAD_DIM, KV_BLK), lambda b, i: (b, 0, 0, 0))
    cache_spec = pl.BlockSpec((None, None, GQA_KV_HEADS, HEAD_DIM, PAST_LEN), lambda b, i: (b, layer, 0, 0, 0))
    return pl.pallas_call(
        _gqa_kernel,
        grid=(DEC_BATCH, tiles),
        in_specs=[pl.BlockSpec((_GQA_TQ, ATTN_W), lambda b, i: (b * tiles + i, 0)),
                  new_spec, new_spec, cache_spec, cache_spec],
        out_specs=pl.BlockSpec((_GQA_TQ, ATTN_W), lambda b, i: (b * tiles + i, 0)),
        out_shape=jax.ShapeDtypeStruct((DEC_BATCH * DEC_SEQ, ATTN_W), _BF16),
        scratch_shapes=[pltpu.VMEM((4, PAIR, _GQA_KEYS), _BF16), pltpu.VMEM((2, PAIR, _GQA_KEYS), _BF16),
                        pltpu.VMEM((1, _GQA_UNIT, _GQA_KEYS), _F32), pltpu.VMEM((1, _GQA_UNIT, _GQA_KEYS), _F32),
                        pltpu.VMEM((1, _GQA_UNIT, _GQA_KEYS), _F32),
                        pltpu.VMEM((1, _GQA_UNIT, _GQA_KEYS), _BF16), pltpu.VMEM((1, _GQA_UNIT, _GQA_KEYS), _BF16)],
        compiler_params=_params(("arbitrary", "arbitrary"), 48),
        name="sample_gqa",
    )(q, k_t, v_t, cache_k_t, cache_v_t)


_NA_GROUP = 4
_NA_GROUPS = GRID_H // _NA_GROUP
_NA_GROUPS_PER_STEP = 4
_NA_WIN_ROWS = 12
_NA_Q = _NA_GROUP * GRID_W
_NA_KEYS = _NA_WIN_ROWS * GRID_W
_NA_WIN_BLOCKS = _NA_KEYS // KV_BLK
_NA_VARIANTS = 3
_NA_OFFSETS = 2 * NA_WIN_R - 1
_NA_RPB_ROWS = 16

assert _NA_GROUP * GRID_W == KV_BLK


def _na_window_start(group):
    lo, hi = 0, GRID_H - _NA_WIN_ROWS
    start = _NA_GROUP * group - NA_WIN_R // 2
    if isinstance(group, int):
        return min(max(start, lo), hi)
    return jnp.clip(start, lo, hi)


def _na_row_plan():
    plan = []
    for group in (0, 1, _NA_GROUPS - 1):
        w0 = _na_window_start(group)
        rows = []
        for i in range(_NA_GROUP):
            r = _NA_GROUP * group + i
            rs = min(max(r - NA_WIN_R // 2, 0), GRID_H - NA_WIN_R)
            rows.append([kr - r + NA_WIN_R - 1 if rs <= kr < rs + NA_WIN_R else None
                         for kr in range(w0, w0 + _NA_WIN_ROWS)])
        plan.append(rows)
    return plan


def _na_build_bias(rpb_ref, bias_ref, t2_ref):
    shape = (GRID_W, LANES)
    c = lax.broadcasted_iota(jnp.int32, shape, 0)
    lane = lax.broadcasted_iota(jnp.int32, shape, 1)
    kc = lane % GRID_W
    cs = jnp.clip(c - NA_WIN_C // 2, 0, GRID_W - NA_WIN_C)
    col_ok = (kc >= cs) & (kc < cs + NA_WIN_C)
    low = lane < GRID_W
    neg = jnp.full(shape, NEG_INF, _F32)
    plan = _na_row_plan()

    def head_body(h, carry):
        for d in range(_NA_OFFSETS):
            v = jnp.broadcast_to(rpb_ref[h, d:d + 1, :], shape)
            lo = pltpu.roll(v, LANES - (NA_WIN_C - 1), 1, stride=1, stride_axis=0)
            hi = pltpu.roll(v, GRID_W - (NA_WIN_C - 1), 1, stride=1, stride_axis=0)
            t2_ref[d] = jnp.where(col_ok, jnp.where(low, lo, hi) * LOG2E, neg)
        for variant in range(_NA_VARIANTS):
            for i in range(_NA_GROUP):
                for m in range(_NA_WIN_ROWS // 2):
                    d_even, d_odd = plan[variant][i][2 * m], plan[variant][i][2 * m + 1]
                    even = neg if d_even is None else t2_ref[d_even]
                    odd = neg if d_odd is None else t2_ref[d_odd]
                    tile = neg if d_even is None and d_odd is None else jnp.where(low, even, odd)
                    bias_ref[variant, h, i * GRID_W:(i + 1) * GRID_W, m * LANES:(m + 1) * LANES] = tile
        return carry

    lax.fori_loop(0, NA_HEADS, head_body, 0)


def _na_kernel(q_ref, k_ref, v_ref, kc_ref, vc_ref, rpb_ref, o_ref, bias_ref, t2_ref,
               s0_ref, s1_ref, s2_ref, p0_ref, p1_ref):
    step = pl.program_id(1)

    @pl.when((pl.program_id(0) == 0) & (step == 0))
    def _():
        _na_build_bias(rpb_ref, bias_ref, t2_ref)

    low_out = _low_half((_NA_Q, LANES))
    s_refs = _staging_views(s0_ref, s1_ref, s2_ref)
    p_refs = _staging_views(p0_ref, p1_ref)
    denoms, outs = {}, {}
    units = [(g, h) for g in range(_NA_GROUPS_PER_STEP) for h in range(NA_HEADS)]

    def geometry(g):
        group = step * _NA_GROUPS_PER_STEP + g
        variant = jnp.where(group == 0, 0, jnp.where(group == _NA_GROUPS - 1, 2, 1))
        return variant, _na_window_start(group) // _NA_GROUP

    geo = [geometry(g) for g in range(_NA_GROUPS_PER_STEP)]

    def scores(u):
        g, h = units[u]
        variant, blk0 = geo[g]
        p, parity = divmod(h, 2)
        q = q_ref[g * _NA_Q:(g + 1) * _NA_Q, p * LANES:(p + 1) * LANES]
        s_ref = s_refs[u % len(s_refs)]
        for w in range(_NA_WIN_BLOCKS):
            lanes = slice(w * KV_BLK, (w + 1) * KV_BLK)
            k = _head_rows(_pair_block(k_ref, (blk0 + w,), p), parity)
            s_ref[:, lanes] = _dot(q, k) + bias_ref[variant, h, :, lanes]
        kc = _head_rows(_pair_block(kc_ref, (), p).astype(_BF16), parity)
        s_ref[:, _NA_KEYS:_NA_KEYS + PAST_LEN] = _dot(q, kc)

    def softmax(u):
        denoms[u] = _softmax_to(s_refs[u % len(s_refs)], p_refs[u % 2])

    def weighted_values(u):
        g, h = units[u]
        _, blk0 = geo[g]
        p = h // 2
        p_ref = p_refs[u % 2]
        pv = _dot_nt(p_ref[:, _NA_KEYS:_NA_KEYS + PAST_LEN], _pair_block(vc_ref, (), p).astype(_BF16))
        for w in range(_NA_WIN_BLOCKS):
            pv = pv + _dot_nt(p_ref[:, w * KV_BLK:(w + 1) * KV_BLK], _pair_block(v_ref, (blk0 + w,), p))
        outs[u] = pv / denoms[u]
        if h % 2 == 1:
            o_ref[g * _NA_Q:(g + 1) * _NA_Q, p * LANES:(p + 1) * LANES] = (
                jnp.where(low_out, outs[u - 1], outs[u]).astype(o_ref.dtype))

    _run_pipeline(len(units), scores, softmax, weighted_values, len(s_refs))


def _sample_na(q, k_t, v_t, cache_k_t, cache_v_t, rpb, layer):
    steps = _NA_GROUPS // _NA_GROUPS_PER_STEP
    tq = _NA_GROUPS_PER_STEP * _NA_Q
    new_spec = pl.BlockSpec((KV_BLOCKS, NA_HEADS, HEAD_DIM, KV_BLK), lambda b, i: (b, 0, 0, 0))
    cache_spec = pl.BlockSpec((None, None, NA_HEADS, HEAD_DIM, PAST_LEN), lambda b, i: (b, layer, 0, 0, 0))
    return pl.pallas_call(
        _na_kernel,
        grid=(DEC_BATCH, steps),
        in_specs=[pl.BlockSpec((tq, ATTN_W), lambda b, i: (b * steps + i, 0)),
                  new_spec, new_spec, cache_spec, cache_spec,
                  pl.BlockSpec((None, NA_HEADS, _NA_RPB_ROWS, LANES), lambda b, i: (layer, 0, 0, 0))],
        out_specs=pl.BlockSpec((tq, ATTN_W), lambda b, i: (b * steps + i, 0)),
        out_shape=jax.ShapeDtypeStruct((DEC_BATCH * DEC_SEQ, ATTN_W), _BF16),
        scratch_shapes=[pltpu.VMEM((_NA_VARIANTS, NA_HEADS, _NA_Q, _NA_KEYS), _F32),
                        pltpu.VMEM((_NA_OFFSETS, GRID_W, LANES), _F32),
                        pltpu.VMEM((1, _NA_Q, _NA_KEYS + PAST_LEN), _F32),
                        pltpu.VMEM((1, _NA_Q, _NA_KEYS + PAST_LEN), _F32),
                        pltpu.VMEM((1, _NA_Q, _NA_KEYS + PAST_LEN), _F32),
                        pltpu.VMEM((1, _NA_Q, _NA_KEYS + PAST_LEN), _BF16),
                        pltpu.VMEM((1, _NA_Q, _NA_KEYS + PAST_LEN), _BF16)],
        compiler_params=_params(("arbitrary", "arbitrary"), 48),
        name="sample_na",
    )(q, k_t, v_t, cache_k_t, cache_v_t, rpb)


def _out_mlp_kernel(*refs, seg_len, halo):
    if halo:
        h_ref, hprev_ref, hnext_ref = refs[:3]
        refs = refs[3:]
    else:
        h_ref = refs[0]
        refs = refs[1:]
    (b_ref, c_ref, x_ref, g1_ref, sh2_ref, sc2_ref, g2_ref, n2_ref, wo_ref, w1_ref, w2_ref,
     cw_ref, cb_ref, cg_ref, cbeta_ref, o_ref, hp_ref, a_ref) = refs
    b0 = CONV_CH
    c0 = CONV_CH + ATTN_W
    tm = x_ref.shape[0]

    seg_stride = seg_len + 2 * _CONV_PAD
    zeros = jnp.zeros((_CONV_PAD, CONV_CH), _F32)
    if halo:
        first_half = pl.program_id(0) % 2 == 0
        before = jnp.where(first_half, zeros, hprev_ref[...].astype(_F32))
        after = jnp.where(first_half, hnext_ref[...].astype(_F32), zeros)
    else:
        before = after = zeros
    for seg in range(tm // seg_len):
        base = seg * seg_stride
        hp_ref[base:base + _CONV_PAD, :] = before
        hp_ref[base + _CONV_PAD:base + _CONV_PAD + seg_len, :] = (
            h_ref[seg * seg_len:(seg + 1) * seg_len, :].astype(_F32))
        hp_ref[base + _CONV_PAD + seg_len:base + seg_stride, :] = after

    def conv_module(rows):
        for r0 in range(rows.start, rows.stop, _CONV_ROWS):
            t0 = (r0 // seg_len) * seg_stride + r0 % seg_len
            y = _conv_chunk(hp_ref, t0, cw_ref, cb_ref, cg_ref, cbeta_ref)
            a_ref[r0:r0 + _CONV_ROWS, :] = y.astype(a_ref.dtype)

    def attention_residual(rows):
        conv_module(rows)
        y = (_dot(a_ref[rows, :], wo_ref[0:b0, :]) + _dot(b_ref[rows, :], wo_ref[b0:c0, :])
             + _dot(c_ref[rows, :], wo_ref[c0:D_MODEL, :]))
        x1 = x_ref[rows, :] + g1_ref[...] * y
        ms = jnp.mean(x1 * x1, axis=-1, keepdims=True)
        h = (x1 * lax.rsqrt(ms + EPS)) * n2_ref[...]
        return x1, (h * (1.0 + sc2_ref[...]) + sh2_ref[...]).astype(_BF16)

    n_sub = tm // _SUB_TILE
    sub_rows = [slice(s * _SUB_TILE, (s + 1) * _SUB_TILE) for s in range(n_sub)]
    firsts = [attention_residual(rows) for rows in sub_rows]
    for rows, (x1, hb) in zip(sub_rows, firsts):
        acc = jnp.zeros(x1.shape, _F32)
        for j in range(D_FF // FF_CHUNK):
            t = jnp.maximum(_dot(hb, w1_ref[:, j * FF_CHUNK:(j + 1) * FF_CHUNK]), 0.0)
            acc = acc + _dot((t * t).astype(_BF16), w2_ref[j * FF_CHUNK:(j + 1) * FF_CHUNK, :])
        o_ref[rows, :] = x1 + g2_ref[...] * acc


def _out_mlp(h, b, c, x, mod4, layer, norm2_g, w_out, w1, w2, dw_w, dw_b, ln_g, ln_b, *, sample):
    tokens = x.shape[0]
    tm = TM
    row = _row_fn(sample, tm)
    seq = DEC_SEQ if sample else SEQ
    seg_len = min(seq, tm)
    halo = seq > tm
    assert tm % seg_len == 0 and (not halo or seq == 2 * tm)

    def tok(w):
        return pl.BlockSpec((tm, w), lambda i: (i, 0))

    def resident(shape):
        return pl.BlockSpec((None,) + shape, lambda i: (layer,) + (0,) * len(shape),
                            pipeline_mode=pl.Buffered(1))

    def vec(w):
        return pl.BlockSpec((None, 1, w), lambda i: (layer, 0, 0))

    in_specs = [tok(CONV_CH)]
    args = [h]
    if halo:
        per_tile = tm // _CONV_PAD
        last = tokens // _CONV_PAD - 1
        in_specs += [pl.BlockSpec((_CONV_PAD, CONV_CH), lambda i: (jnp.maximum(i * per_tile - 1, 0), 0)),
                     pl.BlockSpec((_CONV_PAD, CONV_CH), lambda i: (jnp.minimum((i + 1) * per_tile, last), 0))]
        args += [h, h]
    in_specs += [tok(ATTN_W), tok(ATTN_W), tok(D_MODEL),
                 _mod_spec(layer, row, 2), _mod_spec(layer, row, 3),
                 _mod_spec(layer, row, 4), _mod_spec(layer, row, 5),
                 vec(D_MODEL),
                 resident((D_MODEL, D_MODEL)), resident((D_MODEL, D_FF)), resident((D_FF, D_MODEL)),
                 pl.BlockSpec((None, CONV_K, CONV_CH), lambda i: (layer, 0, 0)),
                 vec(CONV_CH), vec(CONV_CH), vec(CONV_CH)]
    args += [b, c, x, mod4, mod4, mod4, mod4, norm2_g, w_out, w1, w2, dw_w, dw_b, ln_g, ln_b]
    return pl.pallas_call(
        functools.partial(_out_mlp_kernel, seg_len=seg_len, halo=halo),
        grid=(tokens // tm,),
        in_specs=in_specs,
        out_specs=tok(D_MODEL),
        out_shape=jax.ShapeDtypeStruct((tokens, D_MODEL), _F32),
        scratch_shapes=[pltpu.VMEM(((tm // seg_len) * (seg_len + 2 * _CONV_PAD), CONV_CH), _F32),
                        pltpu.VMEM((tm, CONV_CH), _BF16)],
        compiler_params=_params(("arbitrary",), 56),
        name="out_mlp_sample" if sample else "out_mlp_prompt",
    )(*args)


def kernel(x_prompt, x_sample, cache_attn_k, cache_attn_v, cache_na_k, cache_na_v, c, c_ctx, ada_w, ada_b, norm1_g, norm2_g, w_in, conv_dw_w, conv_dw_b, conv_ln_g, conv_ln_b, attn_q_g, attn_k_g, na_q_g, na_k_g, na_rpb, w_out, mlp_w1, mlp_w2):
    cvec = jnp.concatenate(
        [c_ctx[None, :], c, jnp.zeros((MOD_ROWS - 1 - DEC_BATCH, D_MODEL), _F32)], axis=0)
    mod4 = _modulation(cvec, ada_w, ada_b).reshape(DEPTH, MOD_ROWS, 1, 6 * D_MODEL)

    def heads(g, n):
        return jnp.tile(g, (1, n))
    gains = jnp.concatenate(
        [heads(attn_q_g, GQA_HEADS) * Q_SCALE, heads(attn_k_g, GQA_KV_HEADS),
         heads(na_q_g, NA_HEADS) * Q_SCALE, heads(na_k_g, NA_HEADS)], axis=1)[:, None, :]
    rope_tabs = _rope_tables()
    rpb = jnp.pad(na_rpb, ((0, 0), (0, 0), (0, _NA_RPB_ROWS - _NA_OFFSETS),
                           (0, LANES - (2 * NA_WIN_C - 1))))

    w_in_b = w_in.astype(_BF16)
    w_out_b = w_out.astype(_BF16)
    w1_b = mlp_w1.astype(_BF16)
    w2_b = mlp_w2.astype(_BF16)
    n1 = norm1_g.reshape(DEPTH, 1, D_MODEL)
    n2 = norm2_g.reshape(DEPTH, 1, D_MODEL)
    dw_b = conv_dw_b.reshape(DEPTH, 1, CONV_CH)
    ln_g = conv_ln_g.reshape(DEPTH, 1, CONV_CH)
    ln_b = conv_ln_b.reshape(DEPTH, 1, CONV_CH)
    to_t = (0, 1, 3, 4, 2)
    ck_a, cv_a, ck_n, cv_n = (t.transpose(to_t) for t in (cache_attn_k, cache_attn_v, cache_na_k, cache_na_v))

    xp = x_prompt.reshape(BATCH * SEQ, D_MODEL)
    xs = x_sample.reshape(DEC_BATCH * DEC_SEQ, D_MODEL)
    ctx = tuple(jnp.zeros((BATCH, DEPTH, nh, HEAD_DIM, SEQ), _F32)
                for nh in (GQA_KV_HEADS, GQA_KV_HEADS, NA_HEADS, NA_HEADS))
    for layer in range(DEPTH):
        u, qa, ka, va, qn, kn, vn = _in_proj(xp, mod4, layer, n1, w_in_b, gains, None, ctx, sample=False)
        ctx = (ka, va, kn, vn)
        b_out, c_out = _prompt_attention(qa, ka, va, qn, kn, vn, layer)
        xp = _out_mlp(u, b_out, c_out, xp, mod4, layer, n2, w_out_b, w1_b, w2_b,
                      conv_dw_w, dw_b, ln_g, ln_b, sample=False)
        u, qa, ka, va, qn, kn, vn = _in_proj(xs, mod4, layer, n1, w_in_b, gains, rope_tabs, None, sample=True)
        b_out = _sample_gqa(qa, ka, va, ck_a, cv_a, layer)
        c_out = _sample_na(qn, kn, vn, ck_n, cv_n, rpb, layer)
        xs = _out_mlp(u, b_out, c_out, xs, mod4, layer, n2, w_out_b, w1_b, w2_b,
                      conv_dw_w, dw_b, ln_g, ln_b, sample=True)

    outs = [t.transpose(0, 1, 4, 2, 3) for t in ctx]
    return (xp.reshape(BATCH, SEQ, D_MODEL), xs.reshape(DEC_BATCH, DEC_SEQ, D_MODEL), *outs)
```

```python
import functools
import math

import numpy as np
import jax
import jax.numpy as jnp
from jax import lax
from jax.experimental import pallas as pl
from jax.experimental.pallas import tpu as pltpu

D_MODEL = 1024
BATCH = 32
SEQ = 256
DEPTH = 4
DEC_BATCH = 8
DEC_SEQ = 2048
PAST_LEN = 256
GRID_W = 64
GRID_H = DEC_SEQ // GRID_W
HEAD_DIM = 64
CONV_CH = D_MODEL // 4
CONV_K = 31
ATTN_W = (D_MODEL - CONV_CH) // 2
GQA_HEADS = ATTN_W // HEAD_DIM
GQA_KV_HEADS = GQA_HEADS // 3
GQA_GROUP = GQA_HEADS // GQA_KV_HEADS
NA_HEADS = ATTN_W // HEAD_DIM
D_FF = 4 * D_MODEL
NA_WIN_R = 8
NA_WIN_C = 16
ROPE_THETA = 10000.0
EPS = 1e-6
NEG_INF = -1e30

KV_W = GQA_KV_HEADS * HEAD_DIM
IN_W = 2 * CONV_CH + ATTN_W + 2 * KV_W + 3 * ATTN_W
_U0, _QA0, _KA0, _VA0, _QN0, _KN0, _VN0 = 0, 512, 896, 1024, 1152, 1536, 1920
LANES = 128
SUBLANES = 8
MXU_W = 256
MOD_ROWS = 16
LOG2E = math.log2(math.e)
Q_SCALE = HEAD_DIM ** -0.5 * LOG2E
_SUB_TILE = 512
TM = 2 * _SUB_TILE
FF_CHUNK = 512
KV_BLK = 256
KV_BLOCKS = DEC_SEQ // KV_BLK
PAIR = 2 * HEAD_DIM
MIB = 1024 * 1024

assert SEQ == KV_BLK and PAST_LEN == KV_BLK and _SUB_TILE % KV_BLK == 0 and PAIR == LANES
assert DEC_SEQ % TM == 0 and (BATCH * SEQ) % TM == 0

_F32 = jnp.float32
_BF16 = jnp.bfloat16


def _params(semantics, vmem_mib):
    return pltpu.CompilerParams(dimension_semantics=semantics, vmem_limit_bytes=vmem_mib * MIB)


def _dot(a, b):
    return jnp.dot(a, b, preferred_element_type=_F32)


def _dot_nt(a, b):
    return lax.dot_general(a, b, (((1,), (1,)), ((), ())), preferred_element_type=_F32)


def _low_half(shape):
    return lax.broadcasted_iota(jnp.int32, shape, len(shape) - 1) % LANES < HEAD_DIM


def _mod_kernel(c_ref, w_ref, b_ref, o_ref):
    c = c_ref[...]
    s = (c * jax.nn.sigmoid(c)).astype(_BF16)
    o_ref[...] = _dot(s, w_ref[...].astype(_BF16)) + b_ref[...]


def _modulation(cvec, ada_w, ada_b):
    tn = 1536
    return pl.pallas_call(
        _mod_kernel,
        grid=(DEPTH, 6 * D_MODEL // tn),
        in_specs=[
            pl.BlockSpec((MOD_ROWS, D_MODEL), lambda l, j: (0, 0)),
            pl.BlockSpec((None, D_MODEL, tn), lambda l, j: (l, 0, j)),
            pl.BlockSpec((None, 1, tn), lambda l, j: (l, 0, j)),
        ],
        out_specs=pl.BlockSpec((None, MOD_ROWS, tn), lambda l, j: (l, 0, j)),
        out_shape=jax.ShapeDtypeStruct((DEPTH, MOD_ROWS, 6 * D_MODEL), _F32),
        compiler_params=_params(("arbitrary", "arbitrary"), 40),
        name="modulation",
    )(cvec, ada_w, ada_b.reshape(DEPTH, 1, 6 * D_MODEL))


def _mod_spec(layer, row_fn, part):
    return pl.BlockSpec((None, None, 1, D_MODEL), lambda i: (layer, row_fn(i), 0, part))


def _row_fn(sample, tm):
    tiles_per_seq = DEC_SEQ // tm
    if sample:
        return lambda i: 1 + i // tiles_per_seq
    return lambda i: 0


def _rope_tables():
    t = np.arange(DEC_SEQ)
    pos = (t // GRID_W, t % GRID_W)
    half = HEAD_DIM // 2
    quarter = half // 2
    inv = 1.0 / (ROPE_THETA ** (np.arange(quarter) * 2.0 / half))
    lane = np.arange(LANES)
    d = lane % HEAD_DIM
    axis = d // half
    second = (d % half) >= quarter
    p = np.where(axis[None, :] == 0, pos[0][:, None], pos[1][:, None]).astype(np.float64)
    ang = p * inv[d % quarter][None, :]
    cos, sin = np.cos(ang), np.sin(ang)
    sin_next = np.where(second[None, :], 0.0, -sin)
    sin_prev = np.where(second[None, :], sin, 0.0)
    return tuple(jnp.asarray(a, _F32) for a in (cos, sin_next, sin_prev))


_GAIN_W = 2 * ATTN_W + KV_W + ATTN_W


def _in_proj_kernel(*refs, rope, n_alias):
    n_in = 9 if rope else 6
    if rope:
        x_ref, sh_ref, sc_ref, n1_ref, w_ref, g_ref, cos_ref, sn_ref, sp_ref = refs[:n_in]
    else:
        x_ref, sh_ref, sc_ref, n1_ref, w_ref, g_ref = refs[:n_in]
    u_ref, qa_ref, ka_ref, va_ref, qn_ref, kn_ref, vn_ref = refs[n_in + n_alias:]

    r = lax.broadcasted_iota(jnp.int32, (MXU_W, MXU_W), 0) // HEAD_DIM
    c = lax.broadcasted_iota(jnp.int32, (MXU_W, MXU_W), 1) // HEAD_DIM
    seg_mean = jnp.where(r == c, 1.0 / HEAD_DIM, 0.0).astype(_BF16)

    def normed_input(rows):
        x = x_ref[rows, :]
        ms = jnp.mean(x * x, axis=-1, keepdims=True)
        h = (x * lax.rsqrt(ms + EPS)) * n1_ref[...]
        return (h * (1.0 + sc_ref[...]) + sh_ref[...]).astype(_BF16)

    def project(sub, hb):
        rows = slice(sub * _SUB_TILE, (sub + 1) * _SUB_TILE)
        blk0 = sub * (_SUB_TILE // KV_BLK)

        y_all = _dot(hb, w_ref[...])

        def proj(c0, width):
            return y_all[:, c0:c0 + width]

        def rotate(blk):
            return (blk * cos_ref[rows, :] + pltpu.roll(blk, LANES - 16, 1) * sn_ref[rows, :]
                    + pltpu.roll(blk, 16, 1) * sp_ref[rows, :])

        def head_norm(c0, width, gain0, rotary):
            for b in range(width // MXU_W):
                y = proj(c0 + b * MXU_W, MXU_W)
                ss = _dot((y * y).astype(_BF16), seg_mean)
                g0 = gain0 + b * MXU_W
                y = y * lax.rsqrt(ss + EPS) * g_ref[:, g0:g0 + MXU_W]
                for j in range(MXU_W // LANES):
                    blk = y[:, j * LANES:(j + 1) * LANES]
                    yield (b * MXU_W) // LANES + j, rotate(blk) if rotary else blk

        def store_transposed(ref, pair, blk):
            t = blk.T
            for wblk in range(_SUB_TILE // KV_BLK):
                piece = t[:, wblk * KV_BLK:(wblk + 1) * KV_BLK].reshape(2, HEAD_DIM, KV_BLK)
                ref[blk0 + wblk, 2 * pair:2 * pair + 2] = piece.astype(ref.dtype)

        u = proj(_U0, 2 * CONV_CH)
        u_ref[rows, :] = (u[:, 0:CONV_CH] * jax.nn.sigmoid(u[:, CONV_CH:2 * CONV_CH])).astype(u_ref.dtype)
        q_blocks = ATTN_W // LANES
        for j, blk in head_norm(_QA0, ATTN_W + KV_W, 0, rope):
            if j < q_blocks:
                qa_ref[rows, j * LANES:(j + 1) * LANES] = blk.astype(qa_ref.dtype)
            else:
                store_transposed(ka_ref, 0, blk)
        store_transposed(va_ref, 0, proj(_VA0, KV_W))
        for j, blk in head_norm(_QN0, 2 * ATTN_W, ATTN_W + KV_W, False):
            if j < q_blocks:
                qn_ref[rows, j * LANES:(j + 1) * LANES] = blk.astype(qn_ref.dtype)
            else:
                store_transposed(kn_ref, j - q_blocks, blk)
        vn = proj(_VN0, ATTN_W)
        for j in range(ATTN_W // LANES):
            store_transposed(vn_ref, j, vn[:, j * LANES:(j + 1) * LANES])

    n_sub = x_ref.shape[0] // _SUB_TILE
    inputs = [normed_input(slice(s * _SUB_TILE, (s + 1) * _SUB_TILE)) for s in range(n_sub)]
    for s in range(n_sub):
        project(s, inputs[s])


def _in_proj(x, mod4, layer, norm1_g, w_in, gains, rope_tabs, prev_ctx, *, sample):
    tokens = x.shape[0]
    tm = TM
    blocks = tm // KV_BLK
    row = _row_fn(sample, tm)
    in_specs = [
        pl.BlockSpec((tm, D_MODEL), lambda i: (i, 0)),
        _mod_spec(layer, row, 0),
        _mod_spec(layer, row, 1),
        pl.BlockSpec((None, 1, D_MODEL), lambda i: (layer, 0, 0)),
        pl.BlockSpec((None, D_MODEL, IN_W), lambda i: (layer, 0, 0), pipeline_mode=pl.Buffered(1)),
        pl.BlockSpec((None, 1, _GAIN_W), lambda i: (layer, 0, 0)),
    ]
    args = [x, mod4, mod4, norm1_g, w_in, gains]
    aliases = {}
    kv_heads = (GQA_KV_HEADS, GQA_KV_HEADS, NA_HEADS, NA_HEADS)
    if sample:
        tiles_per_seq = DEC_SEQ // tm
        in_specs += [pl.BlockSpec((tm, LANES), lambda i: (i % tiles_per_seq, 0))] * 3
        args += list(rope_tabs)
        kv_specs = [pl.BlockSpec((blocks, nh, HEAD_DIM, KV_BLK), lambda i: (i, 0, 0, 0)) for nh in kv_heads]
        kv_shapes = [jax.ShapeDtypeStruct((tokens // KV_BLK, nh, HEAD_DIM, KV_BLK), _BF16) for nh in kv_heads]
    else:
        kv_specs = [pl.BlockSpec((blocks, None, nh, HEAD_DIM, KV_BLK), lambda i: (i, layer, 0, 0, 0))
                    for nh in kv_heads]
        kv_shapes = [jax.ShapeDtypeStruct((BATCH, DEPTH, nh, HEAD_DIM, SEQ), _F32) for nh in kv_heads]
        kv_out_index = (2, 3, 5, 6)
        for arr, out_idx in zip(prev_ctx, kv_out_index):
            aliases[len(args)] = out_idx
            in_specs.append(pl.BlockSpec(memory_space=pl.ANY))
            args.append(arr)

    def tok(w):
        return pl.BlockSpec((tm, w), lambda i: (i, 0))

    def tok_shape(w):
        return jax.ShapeDtypeStruct((tokens, w), _BF16)

    out_specs = [tok(CONV_CH), tok(ATTN_W), kv_specs[0], kv_specs[1], tok(ATTN_W), kv_specs[2], kv_specs[3]]
    out_shape = [tok_shape(CONV_CH), tok_shape(ATTN_W), kv_shapes[0], kv_shapes[1],
                 tok_shape(ATTN_W), kv_shapes[2], kv_shapes[3]]
    return pl.pallas_call(
        functools.partial(_in_proj_kernel, rope=sample, n_alias=len(aliases)),
        grid=(tokens // tm,),
        in_specs=in_specs,
        out_specs=out_specs,
        out_shape=out_shape,
        input_output_aliases=aliases,
        compiler_params=_params(("arbitrary",), 48),
        name="in_proj_sample" if sample else "in_proj_prompt",
    )(*args)


_CONV_ROWS = 64
_CONV_PAD = 16


def _conv_chunk(hp_ref, t0, w_ref, b_ref, g_ref, beta_ref):
    first = _CONV_PAD - CONV_K // 2
    span = _CONV_ROWS + SUBLANES
    halves = []
    for c0 in range(0, CONV_CH, LANES):
        lanes = slice(c0, c0 + LANES)
        acc = jnp.zeros((_CONV_ROWS, LANES), _F32) + b_ref[:, lanes]
        for s in range(SUBLANES):
            part = None
            for a in range((first + CONV_K + SUBLANES - 1) // SUBLANES):
                k = SUBLANES * a + s - first
                if 0 <= k < CONV_K:
                    term = hp_ref[pl.ds(t0 + SUBLANES * a, span), lanes] * w_ref[k:k + 1, lanes]
                    part = term if part is None else part + term
            acc = acc + part[s:s + _CONV_ROWS, :]
        halves.append(acc)
    acc = jnp.concatenate(halves, axis=1)
    mu = jnp.mean(acc, axis=-1, keepdims=True)
    cen = acc - mu
    var = jnp.mean(cen * cen, axis=-1, keepdims=True)
    y = cen * lax.rsqrt(var + EPS) * g_ref[...] + beta_ref[...]
    return y * jax.nn.sigmoid(y)


_SOFTMAX_ROWS = 16
_SOFTMAX_LANES = 768


def _softmax_to(s_ref, p_ref):
    n, width = s_ref.shape
    sums = []
    for r0 in range(0, n, _SOFTMAX_ROWS):
        rows = slice(r0, r0 + _SOFTMAX_ROWS)
        m = jnp.max(s_ref[rows, :], axis=-1, keepdims=True)
        part = None
        for l0 in range(0, width, _SOFTMAX_LANES):
            l1 = min(l0 + _SOFTMAX_LANES, width)
            e = jnp.exp2(s_ref[rows, l0:l1] - m)
            p_ref[rows, l0:l1] = e.astype(p_ref.dtype)
            for j in range(0, l1 - l0, LANES):
                part = e[:, j:j + LANES] if part is None else part + e[:, j:j + LANES]
        sums.append(jnp.sum(part, axis=-1, keepdims=True))
    return jnp.concatenate(sums, axis=0)


def _staging_views(*refs):
    zero = jnp.minimum(pl.program_id(0), 0)
    return tuple(ref.at[zero] for ref in refs)


def _run_pipeline(n, scores, softmax, weighted_values, score_buffers):
    scores(0)
    softmax(0)
    for u in range(1, min(score_buffers, n)):
        scores(u)
    for u in range(n):
        if u + 1 < n:
            softmax(u + 1)
        weighted_values(u)
        if u + score_buffers < n:
            scores(u + score_buffers)


def _head_rows(block, parity):
    rows = lax.broadcasted_iota(jnp.int32, block.shape, 0)
    keep = rows < HEAD_DIM if parity == 0 else rows >= HEAD_DIM
    return jnp.where(keep, block, jnp.zeros_like(block))


def _pair_block(ref, idx, pair):
    return ref[idx + (slice(2 * pair, 2 * pair + 2),)].reshape(PAIR, KV_BLK)


def _softmax_pv_t(s, value_t):
    m = jnp.max(s, axis=-1, keepdims=True)
    e = jnp.exp2(s - m)
    return _dot_nt(e.astype(_BF16), value_t) / jnp.sum(e, axis=-1, keepdims=True)


_PROMPT_SEQS = 4


def _prompt_attn_kernel(qa_ref, ka_ref, va_ref, qn_ref, kn_ref, vn_ref, b_ref, c_ref):
    low_out = _low_half((SEQ, LANES))
    zero = jnp.zeros((HEAD_DIM, SEQ), _BF16)
    for p in range(GQA_HEADS // 2):
        cols = slice(p * LANES, (p + 1) * LANES)
        for b in range(_PROMPT_SEQS):
            rows = slice(b * SEQ, (b + 1) * SEQ)
            q = qa_ref[rows, cols]
            outs = []
            for parity in range(2):
                kv = (2 * p + parity) // GQA_GROUP
                k = ka_ref[b, kv].astype(_BF16)
                v = va_ref[b, kv].astype(_BF16)
                k_sel = jnp.concatenate([k, zero] if parity == 0 else [zero, k], axis=0)
                outs.append(_softmax_pv_t(_dot(q, k_sel), jnp.concatenate([v, v], axis=0)))
            b_ref[rows, cols] = jnp.where(low_out, outs[0], outs[1]).astype(b_ref.dtype)
            q = qn_ref[rows, cols]
            k = _pair_block(kn_ref, (b,), p).astype(_BF16)
            v = _pair_block(vn_ref, (b,), p).astype(_BF16)
            outs = [_softmax_pv_t(_dot(q, _head_rows(k, parity)), v) for parity in range(2)]
            c_ref[rows, cols] = jnp.where(low_out, outs[0], outs[1]).astype(c_ref.dtype)


def _prompt_attention(qa, ka_t, va_t, qn, kn_t, vn_t, layer):
    n = _PROMPT_SEQS

    def tok(w):
        return pl.BlockSpec((n * SEQ, w), lambda i: (i, 0))

    def ctx(nh):
        return pl.BlockSpec((n, None, nh, HEAD_DIM, SEQ), lambda i: (i, layer, 0, 0, 0))

    return pl.pallas_call(
        _prompt_attn_kernel,
        grid=(BATCH // n,),
        in_specs=[tok(ATTN_W), ctx(GQA_KV_HEADS), ctx(GQA_KV_HEADS), tok(ATTN_W), ctx(NA_HEADS), ctx(NA_HEADS)],
        out_specs=[tok(ATTN_W), tok(ATTN_W)],
        out_shape=[jax.ShapeDtypeStruct((BATCH * SEQ, ATTN_W), _BF16)] * 2,
        compiler_params=_params(("arbitrary",), 32),
        name="prompt_attention",
    )(qa, ka_t, va_t, qn, kn_t, vn_t)


_GQA_TQ = 512
_GQA_UNIT = 256
_GQA_KEYS = DEC_SEQ + PAST_LEN


def _gqa_kernel(q_ref, k_ref, v_ref, kc_ref, vc_ref, o_ref, ksel_ref, vdup_ref,
                s0_ref, s1_ref, s2_ref, p0_ref, p1_ref):
    @pl.when(pl.program_id(1) == 0)
    def _():
        ksel_ref[...] = jnp.zeros(ksel_ref.shape, _BF16)
        for blk in range(KV_BLOCKS + 1):
            lanes = slice(blk * KV_BLK, (blk + 1) * KV_BLK)
            for kv in range(GQA_KV_HEADS):
                if blk < KV_BLOCKS:
                    k, v = k_ref[blk, kv], v_ref[blk, kv]
                else:
                    k, v = kc_ref[kv].astype(_BF16), vc_ref[kv].astype(_BF16)
                ksel_ref[2 * kv, 0:HEAD_DIM, lanes] = k
                ksel_ref[2 * kv + 1, HEAD_DIM:PAIR, lanes] = k
                vdup_ref[kv, 0:HEAD_DIM, lanes] = v
                vdup_ref[kv, HEAD_DIM:PAIR, lanes] = v

    low_out = _low_half((_GQA_UNIT, LANES))
    s_refs = _staging_views(s0_ref, s1_ref, s2_ref)
    p_refs = _staging_views(p0_ref, p1_ref)
    denoms, outs = {}, {}
    units = [(g, h) for g in range(_GQA_TQ // _GQA_UNIT) for h in range(GQA_HEADS)]

    def scores(u):
        g, head = units[u]
        p = head // 2
        kid = 2 * (head // GQA_GROUP) + head % 2
        q = q_ref[g * _GQA_UNIT:(g + 1) * _GQA_UNIT, p * LANES:(p + 1) * LANES]
        s_refs[u % len(s_refs)][...] = _dot(q, ksel_ref[kid])

    def softmax(u):
        denoms[u] = _softmax_to(s_refs[u % len(s_refs)], p_refs[u % 2])

    def weighted_values(u):
        g, head = units[u]
        outs[u] = _dot_nt(p_refs[u % 2][...], vdup_ref[head // GQA_GROUP]) / denoms[u]
        if head % 2 == 1:
            p = head // 2
            o_ref[g * _GQA_UNIT:(g + 1) * _GQA_UNIT, p * LANES:(p + 1) * LANES] = (
                jnp.where(low_out, outs[u - 1], outs[u]).astype(o_ref.dtype))

    _run_pipeline(len(units), scores, softmax, weighted_values, len(s_refs))


def _sample_gqa(q, k_t, v_t, cache_k_t, cache_v_t, layer):
    tiles = DEC_SEQ // _GQA_TQ
    new_spec = pl.BlockSpec((KV_BLOCKS, GQA_KV_HEADS, HEAD_DIM, KV_BLK), lambda b, i: (b, 0, 0, 0))
    cache_spec = pl.BlockSpec((None, None, GQA_KV_HEADS, HEAD_DIM, PAST_LEN), lambda b, i: (b, layer, 0, 0, 0))
    return pl.pallas_call(
        _gqa_kernel,
        grid=(DEC_BATCH, tiles),
        in_specs=[pl.BlockSpec((_GQA_TQ, ATTN_W), lambda b, i: (b * tiles + i, 0)),
                  new_spec, new_spec, cache_spec, cache_spec],
        out_specs=pl.BlockSpec((_GQA_TQ, ATTN_W), lambda b, i: (b * tiles + i, 0)),
        out_shape=jax.ShapeDtypeStruct((DEC_BATCH * DEC_SEQ, ATTN_W), _BF16),
        scratch_shapes=[pltpu.VMEM((4, PAIR, _GQA_KEYS), _BF16), pltpu.VMEM((2, PAIR, _GQA_KEYS), _BF16),
                        pltpu.VMEM((1, _GQA_UNIT, _GQA_KEYS), _F32), pltpu.VMEM((1, _GQA_UNIT, _GQA_KEYS), _F32),
                        pltpu.VMEM((1, _GQA_UNIT, _GQA_KEYS), _F32),
                        pltpu.VMEM((1, _GQA_UNIT, _GQA_KEYS), _BF16), pltpu.VMEM((1, _GQA_UNIT, _GQA_KEYS), _BF16)],
        compiler_params=_params(("arbitrary", "arbitrary"), 48),
        name="sample_gqa",
    )(q, k_t, v_t, cache_k_t, cache_v_t)


_NA_GROUP = 4
_NA_GROUPS = GRID_H // _NA_GROUP
_NA_GROUPS_PER_STEP = 4
_NA_WIN_ROWS = 12
_NA_Q = _NA_GROUP * GRID_W
_NA_KEYS = _NA_WIN_ROWS * GRID_W
_NA_WIN_BLOCKS = _NA_KEYS // KV_BLK
_NA_VARIANTS = 3
_NA_OFFSETS = 2 * NA_WIN_R - 1
_NA_RPB_ROWS = 16

assert _NA_GROUP * GRID_W == KV_BLK


def _na_window_start(group):
    lo, hi = 0, GRID_H - _NA_WIN_ROWS
    start = _NA_GROUP * group - NA_WIN_R // 2
    if isinstance(group, int):
        return min(max(start, lo), hi)
    return jnp.clip(start, lo, hi)


def _na_row_plan():
    plan = []
    for group in (0, 1, _NA_GROUPS - 1):
        w0 = _na_window_start(group)
        rows = []
        for i in range(_NA_GROUP):
            r = _NA_GROUP * group + i
            rs = min(max(r - NA_WIN_R // 2, 0), GRID_H - NA_WIN_R)
            rows.append([kr - r + NA_WIN_R - 1 if rs <= kr < rs + NA_WIN_R else None
                         for kr in range(w0, w0 + _NA_WIN_ROWS)])
        plan.append(rows)
    return plan


def _na_build_bias(rpb_ref, bias_ref, t2_ref):
    shape = (GRID_W, LANES)
    c = lax.broadcasted_iota(jnp.int32, shape, 0)
    lane = lax.broadcasted_iota(jnp.int32, shape, 1)
    kc = lane % GRID_W
    cs = jnp.clip(c - NA_WIN_C // 2, 0, GRID_W - NA_WIN_C)
    col_ok = (kc >= cs) & (kc < cs + NA_WIN_C)
    low = lane < GRID_W
    neg = jnp.full(shape, NEG_INF, _F32)
    plan = _na_row_plan()

    def head_body(h, carry):
        for d in range(_NA_OFFSETS):
            v = jnp.broadcast_to(rpb_ref[h, d:d + 1, :], shape)
            lo = pltpu.roll(v, LANES - (NA_WIN_C - 1), 1, stride=1, stride_axis=0)
            hi = pltpu.roll(v, GRID_W - (NA_WIN_C - 1), 1, stride=1, stride_axis=0)
            t2_ref[d] = jnp.where(col_ok, jnp.where(low, lo, hi) * LOG2E, neg)
        for variant in range(_NA_VARIANTS):
            for i in range(_NA_GROUP):
                for m in range(_NA_WIN_ROWS // 2):
                    d_even, d_odd = plan[variant][i][2 * m], plan[variant][i][2 * m + 1]
                    even = neg if d_even is None else t2_ref[d_even]
                    odd = neg if d_odd is None else t2_ref[d_odd]
                    tile = neg if d_even is None and d_odd is None else jnp.where(low, even, odd)
                    bias_ref[variant, h, i * GRID_W:(i + 1) * GRID_W, m * LANES:(m + 1) * LANES] = tile
        return carry

    lax.fori_loop(0, NA_HEADS, head_body, 0)


def _na_kernel(q_ref, k_ref, v_ref, kc_ref, vc_ref, rpb_ref, o_ref, bias_ref, t2_ref,
               s0_ref, s1_ref, s2_ref, p0_ref, p1_ref):
    step = pl.program_id(1)

    @pl.when((pl.program_id(0) == 0) & (step == 0))
    def _():
        _na_build_bias(rpb_ref, bias_ref, t2_ref)

    low_out = _low_half((_NA_Q, LANES))
    s_refs = _staging_views(s0_ref, s1_ref, s2_ref)
    p_refs = _staging_views(p0_ref, p1_ref)
    denoms, outs = {}, {}
    units = [(g, h) for g in range(_NA_GROUPS_PER_STEP) for h in range(NA_HEADS)]

    def geometry(g):
        group = step * _NA_GROUPS_PER_STEP + g
        variant = jnp.where(group == 0, 0, jnp.where(group == _NA_GROUPS - 1, 2, 1))
        return variant, _na_window_start(group) // _NA_GROUP

    geo = [geometry(g) for g in range(_NA_GROUPS_PER_STEP)]

    def scores(u):
        g, h = units[u]
        variant, blk0 = geo[g]
        p, parity = divmod(h, 2)
        q = q_ref[g * _NA_Q:(g + 1) * _NA_Q, p * LANES:(p + 1) * LANES]
        s_ref = s_refs[u % len(s_refs)]
        keys = [_pair_block(k_ref, (blk0 + w,), p) for w in range(_NA_WIN_BLOCKS)]
        keys.append(_pair_block(kc_ref, (), p).astype(_BF16))
        s = _dot(q, _head_rows(jnp.concatenate(keys, axis=1), parity))
        s_ref[:, 0:_NA_KEYS] = s[:, 0:_NA_KEYS] + bias_ref[variant, h]
        s_ref[:, _NA_KEYS:_NA_KEYS + PAST_LEN] = s[:, _NA_KEYS:_NA_KEYS + PAST_LEN]

    def softmax(u):
        denoms[u] = _softmax_to(s_refs[u % len(s_refs)], p_refs[u % 2])

    def weighted_values(u):
        g, h = units[u]
        _, blk0 = geo[g]
        p = h // 2
        values = [_pair_block(v_ref, (blk0 + w,), p) for w in range(_NA_WIN_BLOCKS)]
        values.append(_pair_block(vc_ref, (), p).astype(_BF16))
        outs[u] = _dot_nt(p_refs[u % 2][...], jnp.concatenate(values, axis=1)) / denoms[u]
        if h % 2 == 1:
            o_ref[g * _NA_Q:(g + 1) * _NA_Q, p * LANES:(p + 1) * LANES] = (
                jnp.where(low_out, outs[u - 1], outs[u]).astype(o_ref.dtype))

    _run_pipeline(len(units), scores, softmax, weighted_values, len(s_refs))


def _sample_na(q, k_t, v_t, cache_k_t, cache_v_t, rpb, layer):
    steps = _NA_GROUPS // _NA_GROUPS_PER_STEP
    tq = _NA_GROUPS_PER_STEP * _NA_Q
    new_spec = pl.BlockSpec((KV_BLOCKS, NA_HEADS, HEAD_DIM, KV_BLK), lambda b, i: (b, 0, 0, 0))
    cache_spec = pl.BlockSpec((None, None, NA_HEADS, HEAD_DIM, PAST_LEN), lambda b, i: (b, layer, 0, 0, 0))
    return pl.pallas_call(
        _na_kernel,
        grid=(DEC_BATCH, steps),
        in_specs=[pl.BlockSpec((tq, ATTN_W), lambda b, i: (b * steps + i, 0)),
                  new_spec, new_spec, cache_spec, cache_spec,
                  pl.BlockSpec((None, NA_HEADS, _NA_RPB_ROWS, LANES), lambda b, i: (layer, 0, 0, 0))],
        out_specs=pl.BlockSpec((tq, ATTN_W), lambda b, i: (b * steps + i, 0)),
        out_shape=jax.ShapeDtypeStruct((DEC_BATCH * DEC_SEQ, ATTN_W), _BF16),
        scratch_shapes=[pltpu.VMEM((_NA_VARIANTS, NA_HEADS, _NA_Q, _NA_KEYS), _F32),
                        pltpu.VMEM((_NA_OFFSETS, GRID_W, LANES), _F32),
                        pltpu.VMEM((1, _NA_Q, _NA_KEYS + PAST_LEN), _F32),
                        pltpu.VMEM((1, _NA_Q, _NA_KEYS + PAST_LEN), _F32),
                        pltpu.VMEM((1, _NA_Q, _NA_KEYS + PAST_LEN), _F32),
                        pltpu.VMEM((1, _NA_Q, _NA_KEYS + PAST_LEN), _BF16),
                        pltpu.VMEM((1, _NA_Q, _NA_KEYS + PAST_LEN), _BF16)],
        compiler_params=_params(("arbitrary", "arbitrary"), 48),
        name="sample_na",
    )(q, k_t, v_t, cache_k_t, cache_v_t, rpb)


def _out_mlp_kernel(*refs, seg_len, halo):
    if halo:
        h_ref, hprev_ref, hnext_ref = refs[:3]
        refs = refs[3:]
    else:
        h_ref = refs[0]
        refs = refs[1:]
    (b_ref, c_ref, x_ref, g1_ref, sh2_ref, sc2_ref, g2_ref, n2_ref, wo_ref, w1_ref, w2_ref,
     cw_ref, cb_ref, cg_ref, cbeta_ref, o_ref, hp_ref, a_ref) = refs
    b0 = CONV_CH
    c0 = CONV_CH + ATTN_W
    tm = x_ref.shape[0]

    seg_stride = seg_len + 2 * _CONV_PAD
    zeros = jnp.zeros((_CONV_PAD, CONV_CH), _F32)
    if halo:
        first_half = pl.program_id(0) % 2 == 0
        before = jnp.where(first_half, zeros, hprev_ref[...].astype(_F32))
        after = jnp.where(first_half, hnext_ref[...].astype(_F32), zeros)
    else:
        before = after = zeros
    for seg in range(tm // seg_len):
        base = seg * seg_stride
        hp_ref[base:base + _CONV_PAD, :] = before
        hp_ref[base + _CONV_PAD:base + _CONV_PAD + seg_len, :] = (
            h_ref[seg * seg_len:(seg + 1) * seg_len, :].astype(_F32))
        hp_ref[base + _CONV_PAD + seg_len:base + seg_stride, :] = after

    def conv_module(rows):
        for r0 in range(rows.start, rows.stop, _CONV_ROWS):
            t0 = (r0 // seg_len) * seg_stride + r0 % seg_len
            y = _conv_chunk(hp_ref, t0, cw_ref, cb_ref, cg_ref, cbeta_ref)
            a_ref[r0:r0 + _CONV_ROWS, 0:b0] = y.astype(a_ref.dtype)

    def attention_residual(rows):
        conv_module(rows)
        a_ref[rows, b0:c0] = b_ref[rows, :]
        a_ref[rows, c0:D_MODEL] = c_ref[rows, :]
        y = _dot(a_ref[rows, :], wo_ref[...])
        x1 = x_ref[rows, :] + g1_ref[...] * y
        ms = jnp.mean(x1 * x1, axis=-1, keepdims=True)
        h = (x1 * lax.rsqrt(ms + EPS)) * n2_ref[...]
        return x1, (h * (1.0 + sc2_ref[...]) + sh2_ref[...]).astype(_BF16)

    n_sub = tm // _SUB_TILE
    sub_rows = [slice(s * _SUB_TILE, (s + 1) * _SUB_TILE) for s in range(n_sub)]
    firsts = [attention_residual(rows) for rows in sub_rows]
    for rows, (x1, hb) in zip(sub_rows, firsts):
        acc = jnp.zeros(x1.shape, _F32)
        for j in range(D_FF // FF_CHUNK):
            t = jnp.maximum(_dot(hb, w1_ref[:, j * FF_CHUNK:(j + 1) * FF_CHUNK]), 0.0)
            acc = acc + _dot((t * t).astype(_BF16), w2_ref[j * FF_CHUNK:(j + 1) * FF_CHUNK, :])
        o_ref[rows, :] = x1 + g2_ref[...] * acc


def _out_mlp(h, b, c, x, mod4, layer, norm2_g, w_out, w1, w2, dw_w, dw_b, ln_g, ln_b, *, sample):
    tokens = x.shape[0]
    tm = TM
    row = _row_fn(sample, tm)
    seq = DEC_SEQ if sample else SEQ
    seg_len = min(seq, tm)
    halo = seq > tm
    assert tm % seg_len == 0 and (not halo or seq == 2 * tm)

    def tok(w):
        return pl.BlockSpec((tm, w), lambda i: (i, 0))

    def resident(shape):
        return pl.BlockSpec((None,) + shape, lambda i: (layer,) + (0,) * len(shape),
                            pipeline_mode=pl.Buffered(1))

    def vec(w):
        return pl.BlockSpec((None, 1, w), lambda i: (layer, 0, 0))

    in_specs = [tok(CONV_CH)]
    args = [h]
    if halo:
        per_tile = tm // _CONV_PAD
        last = tokens // _CONV_PAD - 1
        in_specs += [pl.BlockSpec((_CONV_PAD, CONV_CH), lambda i: (jnp.maximum(i * per_tile - 1, 0), 0)),
                     pl.BlockSpec((_CONV_PAD, CONV_CH), lambda i: (jnp.minimum((i + 1) * per_tile, last), 0))]
        args += [h, h]
    in_specs += [tok(ATTN_W), tok(ATTN_W), tok(D_MODEL),
                 _mod_spec(layer, row, 2), _mod_spec(layer, row, 3),
                 _mod_spec(layer, row, 4), _mod_spec(layer, row, 5),
                 vec(D_MODEL),
                 resident((D_MODEL, D_MODEL)), resident((D_MODEL, D_FF)), resident((D_FF, D_MODEL)),
                 pl.BlockSpec((None, CONV_K, CONV_CH), lambda i: (layer, 0, 0)),
                 vec(CONV_CH), vec(CONV_CH), vec(CONV_CH)]
    args += [b, c, x, mod4, mod4, mod4, mod4, norm2_g, w_out, w1, w2, dw_w, dw_b, ln_g, ln_b]
    return pl.pallas_call(
        functools.partial(_out_mlp_kernel, seg_len=seg_len, halo=halo),
        grid=(tokens // tm,),
        in_specs=in_specs,
        out_specs=tok(D_MODEL),
        out_shape=jax.ShapeDtypeStruct((tokens, D_MODEL), _F32),
        scratch_shapes=[pltpu.VMEM(((tm // seg_len) * (seg_len + 2 * _CONV_PAD), CONV_CH), _F32),
                        pltpu.VMEM((tm, D_MODEL), _BF16)],
        compiler_params=_params(("arbitrary",), 56),
        name="out_mlp_sample" if sample else "out_mlp_prompt",
    )(*args)


def kernel(x_prompt, x_sample, cache_attn_k, cache_attn_v, cache_na_k, cache_na_v, c, c_ctx, ada_w, ada_b, norm1_g, norm2_g, w_in, conv_dw_w, conv_dw_b, conv_ln_g, conv_ln_b, attn_q_g, attn_k_g, na_q_g, na_k_g, na_rpb, w_out, mlp_w1, mlp_w2):
    cvec = jnp.concatenate(
        [c_ctx[None, :], c, jnp.zeros((MOD_ROWS - 1 - DEC_BATCH, D_MODEL), _F32)], axis=0)
    mod4 = _modulation(cvec, ada_w, ada_b).reshape(DEPTH, MOD_ROWS, 1, 6 * D_MODEL)

    def heads(g, n):
        return jnp.tile(g, (1, n))
    gains = jnp.concatenate(
        [heads(attn_q_g, GQA_HEADS) * Q_SCALE, heads(attn_k_g, GQA_KV_HEADS),
         heads(na_q_g, NA_HEADS) * Q_SCALE, heads(na_k_g, NA_HEADS)], axis=1)[:, None, :]
    rope_tabs = _rope_tables()
    rpb = jnp.pad(na_rpb, ((0, 0), (0, 0), (0, _NA_RPB_ROWS - _NA_OFFSETS),
                           (0, LANES - (2 * NA_WIN_C - 1))))

    w_in_b = w_in.astype(_BF16)
    w_out_b = w_out.astype(_BF16)
    w1_b = mlp_w1.astype(_BF16)
    w2_b = mlp_w2.astype(_BF16)
    n1 = norm1_g.reshape(DEPTH, 1, D_MODEL)
    n2 = norm2_g.reshape(DEPTH, 1, D_MODEL)
    dw_b = conv_dw_b.reshape(DEPTH, 1, CONV_CH)
    ln_g = conv_ln_g.reshape(DEPTH, 1, CONV_CH)
    ln_b = conv_ln_b.reshape(DEPTH, 1, CONV_CH)
    to_t = (0, 1, 3, 4, 2)
    ck_a, cv_a, ck_n, cv_n = (t.transpose(to_t) for t in (cache_attn_k, cache_attn_v, cache_na_k, cache_na_v))

    xp = x_prompt.reshape(BATCH * SEQ, D_MODEL)
    xs = x_sample.reshape(DEC_BATCH * DEC_SEQ, D_MODEL)
    ctx = tuple(jnp.zeros((BATCH, DEPTH, nh, HEAD_DIM, SEQ), _F32)
                for nh in (GQA_KV_HEADS, GQA_KV_HEADS, NA_HEADS, NA_HEADS))
    for layer in range(DEPTH):
        u, qa, ka, va, qn, kn, vn = _in_proj(xp, mod4, layer, n1, w_in_b, gains, None, ctx, sample=False)
        ctx = (ka, va, kn, vn)
        b_out, c_out = _prompt_attention(qa, ka, va, qn, kn, vn, layer)
        xp = _out_mlp(u, b_out, c_out, xp, mod4, layer, n2, w_out_b, w1_b, w2_b,
                      conv_dw_w, dw_b, ln_g, ln_b, sample=False)
        u, qa, ka, va, qn, kn, vn = _in_proj(xs, mod4, layer, n1, w_in_b, gains, rope_tabs, None, sample=True)
        b_out = _sample_gqa(qa, ka, va, ck_a, cv_a, layer)
        c_out = _sample_na(qn, kn, vn, ck_n, cv_n, rpb, layer)
        xs = _out_mlp(u, b_out, c_out, xs, mod4, layer, n2, w_out_b, w1_b, w2_b,
                      conv_dw_w, dw_b, ln_g, ln_b, sample=True)

    outs = [t.transpose(0, 1, 4, 2, 3) for t in ctx]
    return (xp.reshape(BATCH, SEQ, D_MODEL), xs.reshape(DEC_BATCH, DEC_SEQ, D_MODEL), *outs)
```

```python
import functools
import math

import numpy as np
import jax
import jax.numpy as jnp
from jax import lax
from jax.experimental import pallas as pl
from jax.experimental.pallas import tpu as pltpu

D_MODEL = 1024
BATCH = 32
SEQ = 256
DEPTH = 4
DEC_BATCH = 8
DEC_SEQ = 2048
PAST_LEN = 256
GRID_W = 64
GRID_H = DEC_SEQ // GRID_W
HEAD_DIM = 64
CONV_CH = D_MODEL // 4
CONV_K = 31
ATTN_W = (D_MODEL - CONV_CH) // 2
GQA_HEADS = ATTN_W // HEAD_DIM
GQA_KV_HEADS = GQA_HEADS // 3
GQA_GROUP = GQA_HEADS // GQA_KV_HEADS
NA_HEADS = ATTN_W // HEAD_DIM
D_FF = 4 * D_MODEL
NA_WIN_R = 8
NA_WIN_C = 16
ROPE_THETA = 10000.0
EPS = 1e-6
NEG_INF = -1e30

KV_W = GQA_KV_HEADS * HEAD_DIM
IN_W = 2 * CONV_CH + ATTN_W + 2 * KV_W + 3 * ATTN_W
_U0, _QA0, _KA0, _VA0, _QN0, _KN0, _VN0 = 0, 512, 896, 1024, 1152, 1536, 1920
LANES = 128
SUBLANES = 8
MXU_W = 256
MOD_ROWS = 16
LOG2E = math.log2(math.e)
Q_SCALE = HEAD_DIM ** -0.5 * LOG2E
_SUB_TILE = 512
TM = 2 * _SUB_TILE
FF_CHUNK = 512
KV_BLK = 256
KV_BLOCKS = DEC_SEQ // KV_BLK
PAIR = 2 * HEAD_DIM
MIB = 1024 * 1024

assert SEQ == KV_BLK and PAST_LEN == KV_BLK and _SUB_TILE % KV_BLK == 0 and PAIR == LANES
assert DEC_SEQ % TM == 0 and (BATCH * SEQ) % TM == 0

_F32 = jnp.float32
_BF16 = jnp.bfloat16


def _params(semantics, vmem_mib):
    return pltpu.CompilerParams(dimension_semantics=semantics, vmem_limit_bytes=vmem_mib * MIB)


def _dot(a, b):
    return jnp.dot(a, b, preferred_element_type=_F32)


def _dot_nt(a, b):
    return lax.dot_general(a, b, (((1,), (1,)), ((), ())), preferred_element_type=_F32)


def _low_half(shape):
    return lax.broadcasted_iota(jnp.int32, shape, len(shape) - 1) % LANES < HEAD_DIM


def _mod_kernel(c_ref, w_ref, b_ref, o_ref):
    c = c_ref[...]
    s = (c * jax.nn.sigmoid(c)).astype(_BF16)
    o_ref[...] = _dot(s, w_ref[...].astype(_BF16)) + b_ref[...]


def _modulation(cvec, ada_w, ada_b):
    tn = 1536
    return pl.pallas_call(
        _mod_kernel,
        grid=(DEPTH, 6 * D_MODEL // tn),
        in_specs=[
            pl.BlockSpec((MOD_ROWS, D_MODEL), lambda l, j: (0, 0)),
            pl.BlockSpec((None, D_MODEL, tn), lambda l, j: (l, 0, j)),
            pl.BlockSpec((None, 1, tn), lambda l, j: (l, 0, j)),
        ],
        out_specs=pl.BlockSpec((None, MOD_ROWS, tn), lambda l, j: (l, 0, j)),
        out_shape=jax.ShapeDtypeStruct((DEPTH, MOD_ROWS, 6 * D_MODEL), _F32),
        compiler_params=_params(("arbitrary", "arbitrary"), 40),
        name="modulation",
    )(cvec, ada_w, ada_b.reshape(DEPTH, 1, 6 * D_MODEL))


def _mod_spec(layer, row_fn, part):
    return pl.BlockSpec((None, None, 1, D_MODEL), lambda i: (layer, row_fn(i), 0, part))


def _row_fn(sample, tm):
    tiles_per_seq = DEC_SEQ // tm
    if sample:
        return lambda i: 1 + i // tiles_per_seq
    return lambda i: 0


def _rope_tables():
    t = np.arange(DEC_SEQ)
    pos = (t // GRID_W, t % GRID_W)
    half = HEAD_DIM // 2
    quarter = half // 2
    inv = 1.0 / (ROPE_THETA ** (np.arange(quarter) * 2.0 / half))
    lane = np.arange(LANES)
    d = lane % HEAD_DIM
    axis = d // half
    second = (d % half) >= quarter
    p = np.where(axis[None, :] == 0, pos[0][:, None], pos[1][:, None]).astype(np.float64)
    ang = p * inv[d % quarter][None, :]
    cos, sin = np.cos(ang), np.sin(ang)
    sin_next = np.where(second[None, :], 0.0, -sin)
    sin_prev = np.where(second[None, :], sin, 0.0)
    return tuple(jnp.asarray(a, _F32) for a in (cos, sin_next, sin_prev))


_GAIN_W = 2 * ATTN_W + KV_W + ATTN_W


def _in_proj_kernel(*refs, rope, n_alias, fill_layers):
    n_in = 9 if rope else 6
    if rope:
        x_ref, sh_ref, sc_ref, n1_ref, w_ref, g_ref, cos_ref, sn_ref, sp_ref = refs[:n_in]
    else:
        x_ref, sh_ref, sc_ref, n1_ref, w_ref, g_ref = refs[:n_in]
    u_ref, qa_ref, ka_ref, va_ref, qn_ref, kn_ref, vn_ref = refs[n_in + n_alias:]
    if fill_layers:
        for ref in (ka_ref, va_ref, kn_ref, vn_ref):
            ref[:, 1:] = jnp.zeros((ref.shape[0], ref.shape[1] - 1) + ref.shape[2:], ref.dtype)

    r = lax.broadcasted_iota(jnp.int32, (MXU_W, MXU_W), 0) // HEAD_DIM
    c = lax.broadcasted_iota(jnp.int32, (MXU_W, MXU_W), 1) // HEAD_DIM
    seg_mean = jnp.where(r == c, 1.0 / HEAD_DIM, 0.0).astype(_BF16)

    def normed_input(rows):
        x = x_ref[rows, :]
        ms = jnp.mean(x * x, axis=-1, keepdims=True)
        h = (x * lax.rsqrt(ms + EPS)) * n1_ref[...]
        return (h * (1.0 + sc_ref[...]) + sh_ref[...]).astype(_BF16)

    def project(sub, hb):
        rows = slice(sub * _SUB_TILE, (sub + 1) * _SUB_TILE)
        blk0 = sub * (_SUB_TILE // KV_BLK)

        y_all = _dot(hb, w_ref[...])

        def proj(c0, width):
            return y_all[:, c0:c0 + width]

        def rotate(blk):
            return (blk * cos_ref[rows, :] + pltpu.roll(blk, LANES - 16, 1) * sn_ref[rows, :]
                    + pltpu.roll(blk, 16, 1) * sp_ref[rows, :])

        def head_norm(c0, width, gain0, rotary):
            for b in range(width // MXU_W):
                y = proj(c0 + b * MXU_W, MXU_W)
                ss = _dot((y * y).astype(_BF16), seg_mean)
                g0 = gain0 + b * MXU_W
                y = y * lax.rsqrt(ss + EPS) * g_ref[:, g0:g0 + MXU_W]
                for j in range(MXU_W // LANES):
                    blk = y[:, j * LANES:(j + 1) * LANES]
                    yield (b * MXU_W) // LANES + j, rotate(blk) if rotary else blk

        def store_transposed(ref, pair, blk):
            t = blk.T
            for wblk in range(_SUB_TILE // KV_BLK):
                piece = t[:, wblk * KV_BLK:(wblk + 1) * KV_BLK].reshape(2, HEAD_DIM, KV_BLK)
                if fill_layers:
                    ref[blk0 + wblk, 0, 2 * pair:2 * pair + 2] = piece.astype(ref.dtype)
                else:
                    ref[blk0 + wblk, 2 * pair:2 * pair + 2] = piece.astype(ref.dtype)

        u = proj(_U0, 2 * CONV_CH)
        u_ref[rows, :] = (u[:, 0:CONV_CH] * jax.nn.sigmoid(u[:, CONV_CH:2 * CONV_CH])).astype(u_ref.dtype)
        q_blocks = ATTN_W // LANES
        for j, blk in head_norm(_QA0, ATTN_W + KV_W, 0, rope):
            if j < q_blocks:
                qa_ref[rows, j * LANES:(j + 1) * LANES] = blk.astype(qa_ref.dtype)
            else:
                store_transposed(ka_ref, 0, blk)
        store_transposed(va_ref, 0, proj(_VA0, KV_W))
        for j, blk in head_norm(_QN0, 2 * ATTN_W, ATTN_W + KV_W, False):
            if j < q_blocks:
                qn_ref[rows, j * LANES:(j + 1) * LANES] = blk.astype(qn_ref.dtype)
            else:
                store_transposed(kn_ref, j - q_blocks, blk)
        vn = proj(_VN0, ATTN_W)
        for j in range(ATTN_W // LANES):
            store_transposed(vn_ref, j, vn[:, j * LANES:(j + 1) * LANES])

    n_sub = x_ref.shape[0] // _SUB_TILE
    inputs = [normed_input(slice(s * _SUB_TILE, (s + 1) * _SUB_TILE)) for s in range(n_sub)]
    for s in range(n_sub):
        project(s, inputs[s])


def _in_proj(x, mod4, layer, norm1_g, w_in, gains, rope_tabs, prev_ctx, *, sample):
    tokens = x.shape[0]
    fill_layers = not sample and prev_ctx is None
    assert not fill_layers or layer == 0
    tm = _SUB_TILE if fill_layers else TM
    blocks = tm // KV_BLK
    row = _row_fn(sample, tm)
    in_specs = [
        pl.BlockSpec((tm, D_MODEL), lambda i: (i, 0)),
        _mod_spec(layer, row, 0),
        _mod_spec(layer, row, 1),
        pl.BlockSpec((None, 1, D_MODEL), lambda i: (layer, 0, 0)),
        pl.BlockSpec((None, D_MODEL, IN_W), lambda i: (layer, 0, 0), pipeline_mode=pl.Buffered(1)),
        pl.BlockSpec((None, 1, _GAIN_W), lambda i: (layer, 0, 0)),
    ]
    args = [x, mod4, mod4, norm1_g, w_in, gains]
    aliases = {}
    kv_heads = (GQA_KV_HEADS, GQA_KV_HEADS, NA_HEADS, NA_HEADS)
    if sample:
        tiles_per_seq = DEC_SEQ // tm
        in_specs += [pl.BlockSpec((tm, LANES), lambda i: (i % tiles_per_seq, 0))] * 3
        args += list(rope_tabs)
        kv_specs = [pl.BlockSpec((blocks, nh, HEAD_DIM, KV_BLK), lambda i: (i, 0, 0, 0)) for nh in kv_heads]
        kv_shapes = [jax.ShapeDtypeStruct((tokens // KV_BLK, nh, HEAD_DIM, KV_BLK), _BF16) for nh in kv_heads]
    else:
        kv_shapes = [jax.ShapeDtypeStruct((BATCH, DEPTH, nh, HEAD_DIM, SEQ), _F32) for nh in kv_heads]
        if fill_layers:
            kv_specs = [pl.BlockSpec((blocks, DEPTH, nh, HEAD_DIM, KV_BLK), lambda i: (i, 0, 0, 0, 0))
                        for nh in kv_heads]
        else:
            kv_specs = [pl.BlockSpec((blocks, None, nh, HEAD_DIM, KV_BLK), lambda i: (i, layer, 0, 0, 0))
                        for nh in kv_heads]
            kv_out_index = (2, 3, 5, 6)
            for arr, out_idx in zip(prev_ctx, kv_out_index):
                aliases[len(args)] = out_idx
                in_specs.append(pl.BlockSpec(memory_space=pl.ANY))
                args.append(arr)

    def tok(w):
        return pl.BlockSpec((tm, w), lambda i: (i, 0))

    def tok_shape(w):
        return jax.ShapeDtypeStruct((tokens, w), _BF16)

    out_specs = [tok(CONV_CH), tok(ATTN_W), kv_specs[0], kv_specs[1], tok(ATTN_W), kv_specs[2], kv_specs[3]]
    out_shape = [tok_shape(CONV_CH), tok_shape(ATTN_W), kv_shapes[0], kv_shapes[1],
                 tok_shape(ATTN_W), kv_shapes[2], kv_shapes[3]]
    return pl.pallas_call(
        functools.partial(_in_proj_kernel, rope=sample, n_alias=len(aliases), fill_layers=fill_layers),
        grid=(tokens // tm,),
        in_specs=in_specs,
        out_specs=out_specs,
        out_shape=out_shape,
        input_output_aliases=aliases,
        compiler_params=_params(("arbitrary",), 48),
        name="in_proj_sample" if sample else "in_proj_prompt",
    )(*args)


_CONV_ROWS = 64
_CONV_PAD = 16


def _conv_chunk(hp_ref, t0, w_ref, b_ref, g_ref, beta_ref):
    first = _CONV_PAD - CONV_K // 2
    span = _CONV_ROWS + SUBLANES
    halves = []
    for c0 in range(0, CONV_CH, LANES):
        lanes = slice(c0, c0 + LANES)
        acc = jnp.zeros((_CONV_ROWS, LANES), _F32) + b_ref[:, lanes]
        for s in range(SUBLANES):
            part = None
            for a in range((first + CONV_K + SUBLANES - 1) // SUBLANES):
                k = SUBLANES * a + s - first
                if 0 <= k < CONV_K:
                    term = hp_ref[pl.ds(t0 + SUBLANES * a, span), lanes] * w_ref[k:k + 1, lanes]
                    part = term if part is None else part + term
            acc = acc + part[s:s + _CONV_ROWS, :]
        halves.append(acc)
    acc = jnp.concatenate(halves, axis=1)
    mu = jnp.mean(acc, axis=-1, keepdims=True)
    cen = acc - mu
    var = jnp.mean(cen * cen, axis=-1, keepdims=True)
    y = cen * lax.rsqrt(var + EPS) * g_ref[...] + beta_ref[...]
    return y * jax.nn.sigmoid(y)


_SOFTMAX_ROWS = 16
_SOFTMAX_LANES = 768


def _softmax_to(s_ref, p_ref):
    n, width = s_ref.shape
    sums = []
    for r0 in range(0, n, _SOFTMAX_ROWS):
        rows = slice(r0, r0 + _SOFTMAX_ROWS)
        m = jnp.max(s_ref[rows, :], axis=-1, keepdims=True)
        part = None
        for l0 in range(0, width, _SOFTMAX_LANES):
            l1 = min(l0 + _SOFTMAX_LANES, width)
            e = jnp.exp2(s_ref[rows, l0:l1] - m)
            p_ref[rows, l0:l1] = e.astype(p_ref.dtype)
            for j in range(0, l1 - l0, LANES):
                part = e[:, j:j + LANES] if part is None else part + e[:, j:j + LANES]
        sums.append(jnp.sum(part, axis=-1, keepdims=True))
    return jnp.concatenate(sums, axis=0)


def _staging_views(*refs):
    zero = jnp.minimum(pl.program_id(0), 0)
    return tuple(ref.at[zero] for ref in refs)


def _run_pipeline(n, scores, softmax, weighted_values, score_buffers):
    scores(0)
    softmax(0)
    for u in range(1, min(score_buffers, n)):
        scores(u)
    for u in range(n):
        if u + 1 < n:
            softmax(u + 1)
        weighted_values(u)
        if u + score_buffers < n:
            scores(u + score_buffers)


def _head_rows(block, parity):
    rows = lax.broadcasted_iota(jnp.int32, block.shape, 0)
    keep = rows < HEAD_DIM if parity == 0 else rows >= HEAD_DIM
    return jnp.where(keep, block, jnp.zeros_like(block))


def _pair_block(ref, idx, pair):
    return ref[idx + (slice(2 * pair, 2 * pair + 2),)].reshape(PAIR, KV_BLK)


def _softmax_pv_t(s, value_t):
    m = jnp.max(s, axis=-1, keepdims=True)
    e = jnp.exp2(s - m)
    return _dot_nt(e.astype(_BF16), value_t) / jnp.sum(e, axis=-1, keepdims=True)


_PROMPT_SEQS = 4


def _prompt_attn_kernel(qa_ref, ka_ref, va_ref, qn_ref, kn_ref, vn_ref, b_ref, c_ref):
    low_out = _low_half((SEQ, LANES))
    zero = jnp.zeros((HEAD_DIM, SEQ), _BF16)
    for p in range(GQA_HEADS // 2):
        cols = slice(p * LANES, (p + 1) * LANES)
        for b in range(_PROMPT_SEQS):
            rows = slice(b * SEQ, (b + 1) * SEQ)
            q = qa_ref[rows, cols]
            outs = []
            for parity in range(2):
                kv = (2 * p + parity) // GQA_GROUP
                k = ka_ref[b, kv].astype(_BF16)
                v = va_ref[b, kv].astype(_BF16)
                k_sel = jnp.concatenate([k, zero] if parity == 0 else [zero, k], axis=0)
                outs.append(_softmax_pv_t(_dot(q, k_sel), jnp.concatenate([v, v], axis=0)))
            b_ref[rows, cols] = jnp.where(low_out, outs[0], outs[1]).astype(b_ref.dtype)
            q = qn_ref[rows, cols]
            k = _pair_block(kn_ref, (b,), p).astype(_BF16)
            v = _pair_block(vn_ref, (b,), p).astype(_BF16)
            outs = [_softmax_pv_t(_dot(q, _head_rows(k, parity)), v) for parity in range(2)]
            c_ref[rows, cols] = jnp.where(low_out, outs[0], outs[1]).astype(c_ref.dtype)


def _prompt_attention(qa, ka_t, va_t, qn, kn_t, vn_t, layer):
    n = _PROMPT_SEQS

    def tok(w):
        return pl.BlockSpec((n * SEQ, w), lambda i: (i, 0))

    def ctx(nh):
        return pl.BlockSpec((n, None, nh, HEAD_DIM, SEQ), lambda i: (i, layer, 0, 0, 0))

    return pl.pallas_call(
        _prompt_attn_kernel,
        grid=(BATCH // n,),
        in_specs=[tok(ATTN_W), ctx(GQA_KV_HEADS), ctx(GQA_KV_HEADS), tok(ATTN_W), ctx(NA_HEADS), ctx(NA_HEADS)],
        out_specs=[tok(ATTN_W), tok(ATTN_W)],
        out_shape=[jax.ShapeDtypeStruct((BATCH * SEQ, ATTN_W), _BF16)] * 2,
        compiler_params=_params(("arbitrary",), 32),
        name="prompt_attention",
    )(qa, ka_t, va_t, qn, kn_t, vn_t)


_GQA_TQ = 512
_GQA_UNIT = 256
_GQA_KEYS = DEC_SEQ + PAST_LEN


def _gqa_kernel(q_ref, k_ref, v_ref, kc_ref, vc_ref, o_ref, ksel_ref, vdup_ref,
                s0_ref, s1_ref, s2_ref, p0_ref, p1_ref):
    @pl.when(pl.program_id(1) == 0)
    def _():
        ksel_ref[...] = jnp.zeros(ksel_ref.shape, _BF16)
        for blk in range(KV_BLOCKS + 1):
            lanes = slice(blk * KV_BLK, (blk + 1) * KV_BLK)
            for kv in range(GQA_KV_HEADS):
                if blk < KV_BLOCKS:
                    k, v = k_ref[blk, kv], v_ref[blk, kv]
                else:
                    k, v = kc_ref[kv].astype(_BF16), vc_ref[kv].astype(_BF16)
                ksel_ref[2 * kv, 0:HEAD_DIM, lanes] = k
                ksel_ref[2 * kv + 1, HEAD_DIM:PAIR, lanes] = k
                vdup_ref[kv, 0:HEAD_DIM, lanes] = v
                vdup_ref[kv, HEAD_DIM:PAIR, lanes] = v

    low_out = _low_half((_GQA_UNIT, LANES))
    s_refs = _staging_views(s0_ref, s1_ref, s2_ref)
    p_refs = _staging_views(p0_ref, p1_ref)
    denoms, outs = {}, {}
    units = [(g, h) for g in range(_GQA_TQ // _GQA_UNIT) for h in range(GQA_HEADS)]

    def scores(u):
        g, head = units[u]
        p = head // 2
        kid = 2 * (head // GQA_GROUP) + head % 2
        q = q_ref[g * _GQA_UNIT:(g + 1) * _GQA_UNIT, p * LANES:(p + 1) * LANES]
        s_refs[u % len(s_refs)][...] = _dot(q, ksel_ref[kid])

    def softmax(u):
        denoms[u] = _softmax_to(s_refs[u % len(s_refs)], p_refs[u % 2])

    def weighted_values(u):
        g, head = units[u]
        outs[u] = _dot_nt(p_refs[u % 2][...], vdup_ref[head // GQA_GROUP]) / denoms[u]
        if head % 2 == 1:
            p = head // 2
            o_ref[g * _GQA_UNIT:(g + 1) * _GQA_UNIT, p * LANES:(p + 1) * LANES] = (
                jnp.where(low_out, outs[u - 1], outs[u]).astype(o_ref.dtype))

    _run_pipeline(len(units), scores, softmax, weighted_values, len(s_refs))


def _sample_gqa(q, k_t, v_t, cache_k_t, cache_v_t, layer):
    tiles = DEC_SEQ // _GQA_TQ
    new_spec = pl.BlockSpec((KV_BLOCKS, GQA_KV_HEADS, HEAD_DIM, KV_BLK), lambda b, i: (b, 0, 0, 0))
    cache_spec = pl.BlockSpec((None, None, GQA_KV_HEADS, HEAD_DIM, PAST_LEN), lambda b, i: (b, layer, 0, 0, 0))
    return pl.pallas_call(
        _gqa_kernel,
        grid=(DEC_BATCH, tiles),
        in_specs=[pl.BlockSpec((_GQA_TQ, ATTN_W), lambda b, i: (b * tiles + i, 0)),
                  new_spec, new_spec, cache_spec, cache_spec],
        out_specs=pl.BlockSpec((_GQA_TQ, ATTN_W), lambda b, i: (b * tiles + i, 0)),
        out_shape=jax.ShapeDtypeStruct((DEC_BATCH * DEC_SEQ, ATTN_W), _BF16),
        scratch_shapes=[pltpu.VMEM((4, PAIR, _GQA_KEYS), _BF16), pltpu.VMEM((2, PAIR, _GQA_KEYS), _BF16),
                        pltpu.VMEM((1, _GQA_UNIT, _GQA_KEYS), _F32), pltpu.VMEM((1, _GQA_UNIT, _GQA_KEYS), _F32),
                        pltpu.VMEM((1, _GQA_UNIT, _GQA_KEYS), _F32),
                        pltpu.VMEM((1, _GQA_UNIT, _GQA_KEYS), _BF16), pltpu.VMEM((1, _GQA_UNIT, _GQA_KEYS), _BF16)],
        compiler_params=_params(("arbitrary", "arbitrary"), 48),
        name="sample_gqa",
    )(q, k_t, v_t, cache_k_t, cache_v_t)


_NA_GROUP = 4
_NA_GROUPS = GRID_H // _NA_GROUP
_NA_GROUPS_PER_STEP = 4
_NA_WIN_ROWS = 12
_NA_Q = _NA_GROUP * GRID_W
_NA_KEYS = _NA_WIN_ROWS * GRID_W
_NA_WIN_BLOCKS = _NA_KEYS // KV_BLK
_NA_VARIANTS = 3
_NA_OFFSETS = 2 * NA_WIN_R - 1
_NA_RPB_ROWS = 16

assert _NA_GROUP * GRID_W == KV_BLK


def _na_window_start(group):
    lo, hi = 0, GRID_H - _NA_WIN_ROWS
    start = _NA_GROUP * group - NA_WIN_R // 2
    if isinstance(group, int):
        return min(max(start, lo), hi)
    return jnp.clip(start, lo, hi)


def _na_row_plan():
    plan = []
    for group in (0, 1, _NA_GROUPS - 1):
        w0 = _na_window_start(group)
        rows = []
        for i in range(_NA_GROUP):
            r = _NA_GROUP * group + i
            rs = min(max(r - NA_WIN_R // 2, 0), GRID_H - NA_WIN_R)
            rows.append([kr - r + NA_WIN_R - 1 if rs <= kr < rs + NA_WIN_R else None
                         for kr in range(w0, w0 + _NA_WIN_ROWS)])
        plan.append(rows)
    return plan


def _na_build_bias(rpb_ref, bias_ref, t2_ref):
    shape = (GRID_W, LANES)
    c = lax.broadcasted_iota(jnp.int32, shape, 0)
    lane = lax.broadcasted_iota(jnp.int32, shape, 1)
    kc = lane % GRID_W
    cs = jnp.clip(c - NA_WIN_C // 2, 0, GRID_W - NA_WIN_C)
    col_ok = (kc >= cs) & (kc < cs + NA_WIN_C)
    low = lane < GRID_W
    neg = jnp.full(shape, NEG_INF, _F32)
    plan = _na_row_plan()

    def head_body(h, carry):
        for d in range(_NA_OFFSETS):
            v = jnp.broadcast_to(rpb_ref[h, d:d + 1, :], shape)
            lo = pltpu.roll(v, LANES - (NA_WIN_C - 1), 1, stride=1, stride_axis=0)
            hi = pltpu.roll(v, GRID_W - (NA_WIN_C - 1), 1, stride=1, stride_axis=0)
            t2_ref[d] = jnp.where(col_ok, jnp.where(low, lo, hi) * LOG2E, neg)
        for variant in range(_NA_VARIANTS):
            for i in range(_NA_GROUP):
                for m in range(_NA_WIN_ROWS // 2):
                    d_even, d_odd = plan[variant][i][2 * m], plan[variant][i][2 * m + 1]
                    even = neg if d_even is None else t2_ref[d_even]
                    odd = neg if d_odd is None else t2_ref[d_odd]
                    tile = neg if d_even is None and d_odd is None else jnp.where(low, even, odd)
                    bias_ref[variant, h, i * GRID_W:(i + 1) * GRID_W, m * LANES:(m + 1) * LANES] = tile
        return carry

    lax.fori_loop(0, NA_HEADS, head_body, 0)


def _na_kernel(q_ref, k_ref, v_ref, kc_ref, vc_ref, rpb_ref, o_ref, bias_ref, t2_ref,
               s0_ref, s1_ref, s2_ref, p0_ref, p1_ref):
    step = pl.program_id(1)

    @pl.when((pl.program_id(0) == 0) & (step == 0))
    def _():
        _na_build_bias(rpb_ref, bias_ref, t2_ref)

    low_out = _low_half((_NA_Q, LANES))
    s_refs = _staging_views(s0_ref, s1_ref, s2_ref)
    p_refs = _staging_views(p0_ref, p1_ref)
    denoms, outs = {}, {}
    units = [(g, h) for g in range(_NA_GROUPS_PER_STEP) for h in range(NA_HEADS)]

    def geometry(g):
        group = step * _NA_GROUPS_PER_STEP + g
        variant = jnp.where(group == 0, 0, jnp.where(group == _NA_GROUPS - 1, 2, 1))
        return variant, _na_window_start(group) // _NA_GROUP

    geo = [geometry(g) for g in range(_NA_GROUPS_PER_STEP)]

    def scores(u):
        g, h = units[u]
        variant, blk0 = geo[g]
        p, parity = divmod(h, 2)
        q = q_ref[g * _NA_Q:(g + 1) * _NA_Q, p * LANES:(p + 1) * LANES]
        s_ref = s_refs[u % len(s_refs)]
        keys = [_pair_block(k_ref, (blk0 + w,), p) for w in range(_NA_WIN_BLOCKS)]
        keys.append(_pair_block(kc_ref, (), p).astype(_BF16))
        s = _dot(q, _head_rows(jnp.concatenate(keys, axis=1), parity))
        s_ref[:, 0:_NA_KEYS] = s[:, 0:_NA_KEYS] + bias_ref[variant, h]
        s_ref[:, _NA_KEYS:_NA_KEYS + PAST_LEN] = s[:, _NA_KEYS:_NA_KEYS + PAST_LEN]

    def softmax(u):
        denoms[u] = _softmax_to(s_refs[u % len(s_refs)], p_refs[u % 2])

    def weighted_values(u):
        g, h = units[u]
        _, blk0 = geo[g]
        p = h // 2
        values = [_pair_block(v_ref, (blk0 + w,), p) for w in range(_NA_WIN_BLOCKS)]
        values.append(_pair_block(vc_ref, (), p).astype(_BF16))
        outs[u] = _dot_nt(p_refs[u % 2][...], jnp.concatenate(values, axis=1)) / denoms[u]
        if h % 2 == 1:
            o_ref[g * _NA_Q:(g + 1) * _NA_Q, p * LANES:(p + 1) * LANES] = (
                jnp.where(low_out, outs[u - 1], outs[u]).astype(o_ref.dtype))

    _run_pipeline(len(units), scores, softmax, weighted_values, len(s_refs))


def _sample_na(q, k_t, v_t, cache_k_t, cache_v_t, rpb, layer):
    steps = _NA_GROUPS // _NA_GROUPS_PER_STEP
    tq = _NA_GROUPS_PER_STEP * _NA_Q
    new_spec = pl.BlockSpec((KV_BLOCKS, NA_HEADS, HEAD_DIM, KV_BLK), lambda b, i: (b, 0, 0, 0))
    cache_spec = pl.BlockSpec((None, None, NA_HEADS, HEAD_DIM, PAST_LEN), lambda b, i: (b, layer, 0, 0, 0))
    return pl.pallas_call(
        _na_kernel,
        grid=(DEC_BATCH, steps),
        in_specs=[pl.BlockSpec((tq, ATTN_W), lambda b, i: (b * steps + i, 0)),
                  new_spec, new_spec, cache_spec, cache_spec,
                  pl.BlockSpec((None, NA_HEADS, _NA_RPB_ROWS, LANES), lambda b, i: (layer, 0, 0, 0))],
        out_specs=pl.BlockSpec((tq, ATTN_W), lambda b, i: (b * steps + i, 0)),
        out_shape=jax.ShapeDtypeStruct((DEC_BATCH * DEC_SEQ, ATTN_W), _BF16),
        scratch_shapes=[pltpu.VMEM((_NA_VARIANTS, NA_HEADS, _NA_Q, _NA_KEYS), _F32),
                        pltpu.VMEM((_NA_OFFSETS, GRID_W, LANES), _F32),
                        pltpu.VMEM((1, _NA_Q, _NA_KEYS + PAST_LEN), _F32),
                        pltpu.VMEM((1, _NA_Q, _NA_KEYS + PAST_LEN), _F32),
                        pltpu.VMEM((1, _NA_Q, _NA_KEYS + PAST_LEN), _F32),
                        pltpu.VMEM((1, _NA_Q, _NA_KEYS + PAST_LEN), _BF16),
                        pltpu.VMEM((1, _NA_Q, _NA_KEYS + PAST_LEN), _BF16)],
        compiler_params=_params(("arbitrary", "arbitrary"), 48),
        name="sample_na",
    )(q, k_t, v_t, cache_k_t, cache_v_t, rpb)


def _out_mlp_kernel(*refs, seg_len, halo):
    if halo:
        h_ref, hprev_ref, hnext_ref = refs[:3]
        refs = refs[3:]
    else:
        h_ref = refs[0]
        refs = refs[1:]
    (b_ref, c_ref, x_ref, g1_ref, sh2_ref, sc2_ref, g2_ref, n2_ref, wo_ref, w1_ref, w2_ref,
     cw_ref, cb_ref, cg_ref, cbeta_ref, o_ref, hp_ref, a_ref) = refs
    b0 = CONV_CH
    c0 = CONV_CH + ATTN_W
    tm = x_ref.shape[0]

    seg_stride = seg_len + 2 * _CONV_PAD
    zeros = jnp.zeros((_CONV_PAD, CONV_CH), _F32)
    if halo:
        first_half = pl.program_id(0) % 2 == 0
        before = jnp.where(first_half, zeros, hprev_ref[...].astype(_F32))
        after = jnp.where(first_half, hnext_ref[...].astype(_F32), zeros)
    else:
        before = after = zeros
    for seg in range(tm // seg_len):
        base = seg * seg_stride
        hp_ref[base:base + _CONV_PAD, :] = before
        hp_ref[base + _CONV_PAD:base + _CONV_PAD + seg_len, :] = (
            h_ref[seg * seg_len:(seg + 1) * seg_len, :].astype(_F32))
        hp_ref[base + _CONV_PAD + seg_len:base + seg_stride, :] = after

    def conv_module(rows):
        for r0 in range(rows.start, rows.stop, _CONV_ROWS):
            t0 = (r0 // seg_len) * seg_stride + r0 % seg_len
            y = _conv_chunk(hp_ref, t0, cw_ref, cb_ref, cg_ref, cbeta_ref)
            a_ref[r0:r0 + _CONV_ROWS, 0:b0] = y.astype(a_ref.dtype)

    def attention_residual(rows):
        conv_module(rows)
        a_ref[rows, b0:c0] = b_ref[rows, :]
        a_ref[rows, c0:D_MODEL] = c_ref[rows, :]
        y = _dot(a_ref[rows, :], wo_ref[...])
        x1 = x_ref[rows, :] + g1_ref[...] * y
        ms = jnp.mean(x1 * x1, axis=-1, keepdims=True)
        h = (x1 * lax.rsqrt(ms + EPS)) * n2_ref[...]
        return x1, (h * (1.0 + sc2_ref[...]) + sh2_ref[...]).astype(_BF16)

    n_sub = tm // _SUB_TILE
    sub_rows = [slice(s * _SUB_TILE, (s + 1) * _SUB_TILE) for s in range(n_sub)]
    firsts = [attention_residual(rows) for rows in sub_rows]
    for rows, (x1, hb) in zip(sub_rows, firsts):
        acc = jnp.zeros(x1.shape, _F32)
        for j in range(D_FF // FF_CHUNK):
            t = jnp.maximum(_dot(hb, w1_ref[:, j * FF_CHUNK:(j + 1) * FF_CHUNK]), 0.0)
            acc = acc + _dot((t * t).astype(_BF16), w2_ref[j * FF_CHUNK:(j + 1) * FF_CHUNK, :])
        o_ref[rows, :] = x1 + g2_ref[...] * acc


def _out_mlp(h, b, c, x, mod4, layer, norm2_g, w_out, w1, w2, dw_w, dw_b, ln_g, ln_b, *, sample):
    tokens = x.shape[0]
    tm = TM
    row = _row_fn(sample, tm)
    seq = DEC_SEQ if sample else SEQ
    seg_len = min(seq, tm)
    halo = seq > tm
    assert tm % seg_len == 0 and (not halo or seq == 2 * tm)

    def tok(w):
        return pl.BlockSpec((tm, w), lambda i: (i, 0))

    def resident(shape):
        return pl.BlockSpec((None,) + shape, lambda i: (layer,) + (0,) * len(shape),
                            pipeline_mode=pl.Buffered(1))

    def vec(w):
        return pl.BlockSpec((None, 1, w), lambda i: (layer, 0, 0))

    in_specs = [tok(CONV_CH)]
    args = [h]
    if halo:
        per_tile = tm // _CONV_PAD
        last = tokens // _CONV_PAD - 1
        in_specs += [pl.BlockSpec((_CONV_PAD, CONV_CH), lambda i: (jnp.maximum(i * per_tile - 1, 0), 0)),
                     pl.BlockSpec((_CONV_PAD, CONV_CH), lambda i: (jnp.minimum((i + 1) * per_tile, last), 0))]
        args += [h, h]
    in_specs += [tok(ATTN_W), tok(ATTN_W), tok(D_MODEL),
                 _mod_spec(layer, row, 2), _mod_spec(layer, row, 3),
                 _mod_spec(layer, row, 4), _mod_spec(layer, row, 5),
                 vec(D_MODEL),
                 resident((D_MODEL, D_MODEL)), resident((D_MODEL, D_FF)), resident((D_FF, D_MODEL)),
                 pl.BlockSpec((None, CONV_K, CONV_CH), lambda i: (layer, 0, 0)),
                 vec(CONV_CH), vec(CONV_CH), vec(CONV_CH)]
    args += [b, c, x, mod4, mod4, mod4, mod4, norm2_g, w_out, w1, w2, dw_w, dw_b, ln_g, ln_b]
    return pl.pallas_call(
        functools.partial(_out_mlp_kernel, seg_len=seg_len, halo=halo),
        grid=(tokens // tm,),
        in_specs=in_specs,
        out_specs=tok(D_MODEL),
        out_shape=jax.ShapeDtypeStruct((tokens, D_MODEL), _F32),
        scratch_shapes=[pltpu.VMEM(((tm // seg_len) * (seg_len + 2 * _CONV_PAD), CONV_CH), _F32),
                        pltpu.VMEM((tm, D_MODEL), _BF16)],
        compiler_params=_params(("arbitrary",), 56),
        name="out_mlp_sample" if sample else "out_mlp_prompt",
    )(*args)


def kernel(x_prompt, x_sample, cache_attn_k, cache_attn_v, cache_na_k, cache_na_v, c, c_ctx, ada_w, ada_b, norm1_g, norm2_g, w_in, conv_dw_w, conv_dw_b, conv_ln_g, conv_ln_b, attn_q_g, attn_k_g, na_q_g, na_k_g, na_rpb, w_out, mlp_w1, mlp_w2):
    cvec = jnp.concatenate(
        [c_ctx[None, :], c, jnp.zeros((MOD_ROWS - 1 - DEC_BATCH, D_MODEL), _F32)], axis=0)
    mod4 = _modulation(cvec, ada_w, ada_b).reshape(DEPTH, MOD_ROWS, 1, 6 * D_MODEL)

    def heads(g, n):
        return jnp.tile(g, (1, n))
    gains = jnp.concatenate(
        [heads(attn_q_g, GQA_HEADS) * Q_SCALE, heads(attn_k_g, GQA_KV_HEADS),
         heads(na_q_g, NA_HEADS) * Q_SCALE, heads(na_k_g, NA_HEADS)], axis=1)[:, None, :]
    rope_tabs = _rope_tables()
    rpb = jnp.pad(na_rpb, ((0, 0), (0, 0), (0, _NA_RPB_ROWS - _NA_OFFSETS),
                           (0, LANES - (2 * NA_WIN_C - 1))))

    w_in_b = w_in.astype(_BF16)
    w_out_b = w_out.astype(_BF16)
    w1_b = mlp_w1.astype(_BF16)
    w2_b = mlp_w2.astype(_BF16)
    n1 = norm1_g.reshape(DEPTH, 1, D_MODEL)
    n2 = norm2_g.reshape(DEPTH, 1, D_MODEL)
    dw_b = conv_dw_b.reshape(DEPTH, 1, CONV_CH)
    ln_g = conv_ln_g.reshape(DEPTH, 1, CONV_CH)
    ln_b = conv_ln_b.reshape(DEPTH, 1, CONV_CH)
    to_t = (0, 1, 3, 4, 2)
    ck_a, cv_a, ck_n, cv_n = (t.transpose(to_t) for t in (cache_attn_k, cache_attn_v, cache_na_k, cache_na_v))

    xp = x_prompt.reshape(BATCH * SEQ, D_MODEL)
    xs = x_sample.reshape(DEC_BATCH * DEC_SEQ, D_MODEL)
    ctx = None
    for layer in range(DEPTH):
        u, qa, ka, va, qn, kn, vn = _in_proj(xp, mod4, layer, n1, w_in_b, gains, None, ctx, sample=False)
        ctx = (ka, va, kn, vn)
        b_out, c_out = _prompt_attention(qa, ka, va, qn, kn, vn, layer)
        xp = _out_mlp(u, b_out, c_out, xp, mod4, layer, n2, w_out_b, w1_b, w2_b,
                      conv_dw_w, dw_b, ln_g, ln_b, sample=False)
        u, qa, ka, va, qn, kn, vn = _in_proj(xs, mod4, layer, n1, w_in_b, gains, rope_tabs, None, sample=True)
        b_out = _sample_gqa(qa, ka, va, ck_a, cv_a, layer)
        c_out = _sample_na(qn, kn, vn, ck_n, cv_n, rpb, layer)
        xs = _out_mlp(u, b_out, c_out, xs, mod4, layer, n2, w_out_b, w1_b, w2_b,
                      conv_dw_w, dw_b, ln_g, ln_b, sample=True)

    outs = [t.transpose(0, 1, 4, 2, 3) for t in ctx]
    return (xp.reshape(BATCH, SEQ, D_MODEL), xs.reshape(DEC_BATCH, DEC_SEQ, D_MODEL), *outs)
```
